```python
import jax, jax.numpy as jnp
from jax import lax
import numpy as np

D_MODEL = 1024
BATCH = 32
SEQ = 2048
DEPTH = 2

F32 = jnp.float32
CTX_LEN = 256
GRID_W = 64
BRANCH_WIDTH = D_MODEL // 2
N_BRANCH = 3
POOL_GROUPS = 4
POOL_WINDOWS = (2, 4, 8, 16)
POOL_WIDTH = BRANCH_WIDTH
POOL_GDIM = POOL_WIDTH // POOL_GROUPS
M_HEADS = 4
M_HDIM = BRANCH_WIDTH // M_HEADS
M_WIDTH = M_HEADS * M_HDIM
M_CHUNK = 128
A_HDIM = 64
A_Q_HEADS = BRANCH_WIDTH // A_HDIM
A_KV_HEADS = A_Q_HEADS // 4
A_GROUP = A_Q_HEADS // A_KV_HEADS
A_Q_WIDTH = A_Q_HEADS * A_HDIM
A_KV_WIDTH = A_KV_HEADS * A_HDIM
WINDOW = 128
A_BLOCK = 128
ROPE_BASE = 10000.0
ROPE_PAIRS = A_HDIM // 4
D_FF = ((8 * D_MODEL + 3 * 256 - 1) // (3 * 256)) * 256
SPLIT_SIZES = (POOL_WIDTH, M_WIDTH, M_WIDTH, M_WIDTH, M_WIDTH, 2 * M_HEADS, 2 * M_HEADS,
               A_Q_WIDTH, A_KV_WIDTH, A_KV_WIDTH, N_BRANCH * D_MODEL)
IN_WIDTH = sum(SPLIT_SIZES)
DN_ALPHA = (2 * DEPTH) ** 0.25
DN_BETA = (8 * DEPTH) ** -0.25
LN_EPS = 1e-5

kernel_name = 'hybrid_pool_mlstm_swa_dit_trunk'


def _layer_norm(x, g=None, b=None):
    xf = x.astype(F32)
    mu = jnp.mean(xf, -1, keepdims=True)
    var = jnp.mean(jnp.square(xf - mu), -1, keepdims=True)
    y = (xf - mu) * lax.rsqrt(var + LN_EPS)
    if g is not None:
        y = y * g.astype(F32) + b.astype(F32)
    return y.astype(x.dtype)


def _modulate(x, shift, scale):
    return _layer_norm(x) * (1.0 + scale) + shift


def _post_norm(x, y, g, b):
    return _layer_norm(DN_ALPHA * x + y, g, b)


def _swiglu(h, p):
    return (jax.nn.silu(h @ p['w_ffn_gate']) * (h @ p['w_ffn_up'])) @ p['w_ffn_down']


def _project(h, p):
    B, T, _ = h.shape
    offs = np.cumsum(SPLIT_SIZES)[:-1].tolist()
    (a, qm, km, vm, om, ig, fg, qa, ka, va, gates) = jnp.split(h @ p['w_in'], offs, axis=-1)

    def heads(t):
        return t.reshape(B, T, M_HEADS, M_HDIM).transpose(0, 2, 1, 3).astype(F32)

    m_i = (ig.reshape(B, T, 2, M_HEADS).astype(F32) + p['m_b_i']).transpose(2, 0, 3, 1)
    m_f = jax.nn.log_sigmoid(fg.reshape(B, T, 2, M_HEADS).astype(F32) + p['m_b_f']).transpose(2, 0, 3, 1)
    return {
        'pool': a,
        'm_q': heads(qm), 'm_k': heads(km) * (M_HDIM ** -0.5), 'm_v': heads(vm), 'm_o': om,
        'm_i': m_i, 'm_f': m_f,
        'a_q': qa.reshape(B, T, A_Q_HEADS, A_HDIM),
        'a_k': ka.reshape(B, T, A_KV_HEADS, A_HDIM),
        'a_v': va.reshape(B, T, A_KV_HEADS, A_HDIM),
        'gates': gates,
    }


def _pool_mixer(a, pool_w, pool_scale):
    B, T, _ = a.shape
    af = a.astype(F32)
    csum = jnp.concatenate([jnp.zeros((B, 1, POOL_WIDTH), F32), jnp.cumsum(af, axis=1)], axis=1)
    t = jnp.arange(T)
    outs = []
    for g, w in enumerate(POOL_WINDOWS):
        lo = jnp.clip(t - w // 2, 0, T)
        hi = jnp.clip(t - w // 2 + w, 0, T)
        sl = slice(g * POOL_GDIM, (g + 1) * POOL_GDIM)
        mean = (csum[:, hi, sl] - csum[:, lo, sl]) / (hi - lo).astype(F32)[None, :, None]
        outs.append(mean - af[:, :, sl])
    u = jnp.stack(outs, axis=2).astype(a.dtype)
    u = jnp.einsum('btgc,gcd->btgd', u, pool_w).reshape(B, T, POOL_WIDTH)
    return u * pool_scale


def _mlstm_chunk_step(state, xs):
    C, n, m = state
    q, k, v, ig, lf = xs
    L = q.shape[2]
    b = jnp.cumsum(lf, axis=-1)
    causal = jnp.tril(jnp.ones((L, L), bool))
    log_d = jnp.where(causal, b[..., :, None] - b[..., None, :] + ig[..., None, :], -jnp.inf)
    inter = b + m[..., None]
    m_row = jnp.maximum(inter, jnp.max(log_d, axis=-1))
    s = jnp.einsum('bhjd,bhtd->bhjt', q, k) * jnp.exp(log_d - m_row[..., None])
    w_inter = jnp.exp(inter - m_row)
    num = jnp.einsum('bhjt,bhtv->bhjv', s, v) + w_inter[..., None] * jnp.einsum('bhjd,bhdv->bhjv', q, C)
    nq = jnp.sum(s, axis=-1) + w_inter * jnp.einsum('bhjd,bhd->bhj', q, n)
    den = jnp.maximum(jnp.abs(nq), jnp.exp(-m_row))
    h = num / den[..., None]
    b_last = b[..., -1]
    g = b_last[..., None] - b + ig
    m_new = jnp.maximum(b_last + m, jnp.max(g, axis=-1))
    wk = jnp.exp(g - m_new[..., None])
    decay = jnp.exp(b_last + m - m_new)
    C_new = decay[..., None, None] * C + jnp.einsum('bht,bhtd,bhtv->bhdv', wk, k, v)
    n_new = decay[..., None] * n + jnp.einsum('bht,bhtd->bhd', wk, k)
    return (C_new, n_new, m_new), h


def _mlstm_scan(q, k, v, ig, lf, state):
    B, H, T, d = q.shape
    nc = T // M_CHUNK

    def chunks(a):
        return jnp.moveaxis(a.reshape(a.shape[:2] + (nc, M_CHUNK) + a.shape[3:]), 2, 0)

    state, h = lax.scan(_mlstm_chunk_step, state, (chunks(q), chunks(k), chunks(v), chunks(ig), chunks(lf)))
    return jnp.moveaxis(h, 0, 2).reshape(B, H, T, d), state


def _direction_inputs(pt, direction):
    parts = (pt['m_q'], pt['m_k'], pt['m_v'], pt['m_i'][direction], pt['m_f'][direction])
    if direction == 1:
        return tuple(jnp.flip(a, axis=2) for a in parts)
    return parts


def _mlstm_bidirectional(pl, pc):
    B = pl['m_q'].shape[0]
    zero = (jnp.zeros((B, M_HEADS, M_HDIM, M_HDIM), F32), jnp.zeros((B, M_HEADS, M_HDIM), F32),
            jnp.zeros((B, M_HEADS), F32))
    h_lat, h_ctx = None, None
    for direction in range(2):
        hc, state = _mlstm_scan(*_direction_inputs(pc, direction), zero)
        hl, _ = _mlstm_scan(*_direction_inputs(pl, direction), state)
        if direction == 1:
            hc, hl = jnp.flip(hc, axis=2), jnp.flip(hl, axis=2)
        h_lat = hl if h_lat is None else h_lat + hl
        h_ctx = hc if h_ctx is None else h_ctx + hc
    return h_lat, h_ctx


def _mlstm_readout(h, o, norm_w):
    B, H, T, d = h.shape
    mu = jnp.mean(h, -1, keepdims=True)
    var = jnp.mean(jnp.square(h - mu), -1, keepdims=True)
    hn = ((h - mu) * lax.rsqrt(var + LN_EPS)).transpose(0, 2, 1, 3).reshape(B, T, M_WIDTH)
    return (hn * norm_w.astype(F32) * jax.nn.sigmoid(o.astype(F32))).astype(o.dtype)


def _axial_rope_tables(n_tokens):
    rows = n_tokens // GRID_W
    row = jnp.repeat(jnp.arange(rows, dtype=F32), GRID_W)
    col = jnp.tile(jnp.arange(GRID_W, dtype=F32), rows)
    inv = ROPE_BASE ** (-jnp.arange(ROPE_PAIRS, dtype=F32) / ROPE_PAIRS)
    ang = jnp.concatenate([row[:, None] * inv, col[:, None] * inv], axis=-1)
    return jnp.cos(ang), jnp.sin(ang)


def _apply_axial_rope(x, cos, sin):
    B, T, H, d = x.shape
    xr = x.astype(F32).reshape(B, T, H, 2, 2, ROPE_PAIRS)
    x1, x2 = xr[..., 0, :], xr[..., 1, :]
    c = cos.reshape(T, 1, 2, ROPE_PAIRS)
    s = sin.reshape(T, 1, 2, ROPE_PAIRS)
    out = jnp.stack([x1 * c - x2 * s, x2 * c + x1 * s], axis=-2)
    return out.reshape(B, T, H, d).astype(x.dtype)


def _sink_attend(q, ks, vs, masks, sink):
    scale = A_HDIM ** -0.5
    logits = []
    for k, msk in zip(ks, masks):
        s = jnp.einsum('bqhgd,bkhd->bhgqk', q, k).astype(F32) * scale
        if msk is not None:
            s = jnp.where(msk, s, -jnp.inf)
        logits.append(s)
    B, Q = q.shape[:2]
    logits.append(jnp.broadcast_to(sink.astype(F32)[None, :, :, None, None], (B,) + sink.shape + (Q, 1)))
    p = jax.nn.softmax(jnp.concatenate(logits, axis=-1), axis=-1)
    out, off = None, 0
    for v, s in zip(vs, logits[:-1]):
        nk = s.shape[-1]
        o = jnp.einsum('bhgqk,bkhd->bqhgd', p[..., off:off + nk].astype(v.dtype), v)
        out = o if out is None else out + o
        off += nk
    return out


def _windowed_attention(q, k, v, kc, vc, sink):
    B, T = q.shape[:2]
    nb = T // A_BLOCK
    band = 3 * A_BLOCK
    pad = ((0, 0), (A_BLOCK, A_BLOCK), (0, 0), (0, 0))
    kp, vp = jnp.pad(k, pad), jnp.pad(v, pad)

    def block(i):
        start = i * A_BLOCK
        qb = lax.dynamic_slice_in_dim(q, start, A_BLOCK, axis=1)
        kb = lax.dynamic_slice_in_dim(kp, start, band, axis=1)
        vb = lax.dynamic_slice_in_dim(vp, start, band, axis=1)
        qpos = start + jnp.arange(A_BLOCK)
        kpos = start - A_BLOCK + jnp.arange(band)
        valid = ((jnp.abs(qpos[:, None] - kpos[None, :]) <= WINDOW)
                 & (kpos >= 0)[None, :] & (kpos < T)[None, :])
        return _sink_attend(qb, (kb, kc), (vb, vc), (valid, None), sink)

    out = lax.map(block, jnp.arange(nb))
    return jnp.moveaxis(out, 0, 1).reshape(B, T, A_Q_WIDTH)


def _merge(gates, branch_outs, p):
    B, T, _ = gates.shape
    g = jax.nn.sigmoid(gates.astype(F32)).astype(gates.dtype).reshape(B, T, N_BRANCH, D_MODEL)
    y = None
    for i, u in enumerate(branch_outs):
        yi = g[:, :, i] * (u @ p['w_branch'][i])
        y = yi if y is None else y + yi
    return y @ p['w_out']


def _trunk_layer(x, ctx, mod_x, mod_c, p, cos, sin, last):
    sh1, sc1, g1, sh2, sc2, g2 = jnp.split(mod_x, 6, axis=-1)
    csh1, csc1, cg1, csh2, csc2, cg2 = jnp.split(mod_c, 6, axis=-1)
    B, T, _ = x.shape
    px = _project(_modulate(x, sh1, sc1), p)
    pc = _project(_modulate(ctx, csh1, csc1), p)
    pool_x = _pool_mixer(px['pool'], p['pool_w'], p['pool_scale'])
    h_lat, h_ctx = _mlstm_bidirectional(px, pc)
    mlstm_x = _mlstm_readout(h_lat, px['m_o'], p['m_norm_w'])
    qx = _apply_axial_rope(px['a_q'], cos, sin).reshape(B, T, A_KV_HEADS, A_GROUP, A_HDIM)
    kx = _apply_axial_rope(px['a_k'], cos, sin)
    attn_x = _windowed_attention(qx, kx, px['a_v'], pc['a_k'], pc['a_v'], p['attn_sink'])
    mix_x = _merge(px['gates'], (pool_x, mlstm_x, attn_x), p)
    x = _post_norm(x, g1 * mix_x, p['ln1_g'], p['ln1_b'])
    x = _post_norm(x, g2 * _swiglu(_modulate(x, sh2, sc2), p), p['ln2_g'], p['ln2_b'])
    if not last:
        Bc, Lc, _ = ctx.shape
        pool_c = _pool_mixer(pc['pool'], p['pool_w'], p['pool_scale'])
        mlstm_c = _mlstm_readout(h_ctx, pc['m_o'], p['m_norm_w'])
        qc = pc['a_q'].reshape(Bc, Lc, A_KV_HEADS, A_GROUP, A_HDIM)
        attn_c = _sink_attend(qc, (pc['a_k'],), (pc['a_v'],), (None,), p['attn_sink']).reshape(Bc, Lc, A_Q_WIDTH)
        mix_c = _merge(pc['gates'], (pool_c, mlstm_c, attn_c), p)
        ctx = _post_norm(ctx, cg1 * mix_c, p['ln1_g'], p['ln1_b'])
        ctx = _post_norm(ctx, cg2 * _swiglu(_modulate(ctx, csh2, csc2), p), p['ln2_g'], p['ln2_b'])
    return x, ctx


def setup_inputs(seed: int = 0) -> dict:
    key = jax.random.key(seed)
    ks = jax.random.split(key, 24)
    L = DEPTH

    def nrm(k, shape, s):
        return jax.random.normal(k, shape, F32) * s

    return {
        'x': nrm(ks[0], (BATCH, SEQ, D_MODEL), 1.0),
        'c': nrm(ks[1], (BATCH, D_MODEL), 1.0),
        'ctx': nrm(ks[2], (BATCH, CTX_LEN, D_MODEL), 1.0),
        'c_ctx': nrm(ks[3], (D_MODEL,), 1.0),
        'w_mod': nrm(ks[4], (L, D_MODEL, 6 * D_MODEL), 0.5 * D_MODEL ** -0.5),
        'b_mod': nrm(ks[5], (L, 6 * D_MODEL), 0.01),
        'w_in': nrm(ks[6], (L, D_MODEL, IN_WIDTH), D_MODEL ** -0.5),
        'pool_w': nrm(ks[7], (L, POOL_GROUPS, POOL_GDIM, POOL_GDIM), POOL_GDIM ** -0.5),
        'pool_scale': 1.0 + nrm(ks[8], (L, POOL_WIDTH), 0.02),
        'm_b_i': nrm(ks[9], (L, 2, M_HEADS), 0.1),
        'm_b_f': jnp.linspace(3.0, 6.0, M_HEADS, dtype=F32) + nrm(ks[10], (L, 2, M_HEADS), 0.1),
        'm_norm_w': 1.0 + nrm(ks[11], (L, M_WIDTH), 0.02),
        'attn_sink': nrm(ks[12], (L, A_KV_HEADS, A_GROUP), 0.5),
        'w_branch': nrm(ks[13], (L, N_BRANCH, BRANCH_WIDTH, D_MODEL), BRANCH_WIDTH ** -0.5),
        'w_out': nrm(ks[14], (L, D_MODEL, D_MODEL), DN_BETA * D_MODEL ** -0.5),
        'ln1_g': 1.0 + nrm(ks[15], (L, D_MODEL), 0.02),
        'ln1_b': nrm(ks[16], (L, D_MODEL), 0.02),
        'ln2_g': 1.0 + nrm(ks[17], (L, D_MODEL), 0.02),
        'ln2_b': nrm(ks[18], (L, D_MODEL), 0.02),
        'w_ffn_gate': nrm(ks[19], (L, D_MODEL, D_FF), D_MODEL ** -0.5),
        'w_ffn_up': nrm(ks[20], (L, D_MODEL, D_FF), D_MODEL ** -0.5),
        'w_ffn_down': nrm(ks[21], (L, D_FF, D_MODEL), DN_BETA * D_FF ** -0.5),
    }


def reference(x, c, ctx, c_ctx, w_mod, b_mod, w_in, pool_w, pool_scale, m_b_i, m_b_f, m_norm_w,
              attn_sink, w_branch, w_out, ln1_g, ln1_b, ln2_g, ln2_b, w_ffn_gate, w_ffn_up, w_ffn_down):
    cos, sin = _axial_rope_tables(x.shape[1])
    silu_c = jax.nn.silu(c)
    silu_cc = jax.nn.silu(c_ctx)
    for l in range(DEPTH):
        p = {
            'w_in': w_in[l], 'pool_w': pool_w[l], 'pool_scale': pool_scale[l],
            'm_b_i': m_b_i[l], 'm_b_f': m_b_f[l], 'm_norm_w': m_norm_w[l], 'attn_sink': attn_sink[l],
            'w_branch': w_branch[l], 'w_out': w_out[l],
            'ln1_g': ln1_g[l], 'ln1_b': ln1_b[l], 'ln2_g': ln2_g[l], 'ln2_b': ln2_b[l],
            'w_ffn_gate': w_ffn_gate[l], 'w_ffn_up': w_ffn_up[l], 'w_ffn_down': w_ffn_down[l],
        }
        mod_x = (silu_c @ w_mod[l] + b_mod[l])[:, None, :]
        mod_c = silu_cc @ w_mod[l] + b_mod[l]
        x, ctx = _trunk_layer(x, ctx, mod_x, mod_c, p, cos, sin, l == DEPTH - 1)
    return x
```

```python
import functools

import jax
import jax.numpy as jnp
import numpy as np
from jax import lax
from jax.experimental import pallas as pl
from jax.experimental.pallas import tpu as pltpu

F32 = jnp.float32
BF16 = jnp.bfloat16

GRID_W = 64
N_BRANCH = 3
POOL_WINDOWS = (2, 4, 8, 16)
POOL_GDIM = 128
M_HEADS = 4
M_HDIM = 128
M_CHUNK = 128
A_HDIM = 64
A_Q_HEADS = 8
A_KV_HEADS = 2
A_GROUP = 4
WINDOW = 128
A_BLOCK = 128
ROPE_BASE = 10000.0
ROPE_PAIRS = 16
BW = 512
LN_EPS = 1e-5
NEG = -1e30
LANES = 128
HALO = 16
VMEM_LIMIT_BYTES = 56 * 1024 * 1024


def _cparams(n_axes):
    return pltpu.CompilerParams(dimension_semantics=("arbitrary",) * n_axes,
                                vmem_limit_bytes=VMEM_LIMIT_BYTES)


def _dot(a, b):
    return jnp.dot(a, b, preferred_element_type=F32)


def _dot_nt(a, b):
    return lax.dot_general(a, b, (((1,), (1,)), ((), ())), preferred_element_type=F32)


def _dot_tn(a, b):
    return lax.dot_general(a, b, (((0,), (0,)), ((), ())), preferred_element_type=F32)


def _layer_norm(x):
    mu = jnp.mean(x, axis=-1, keepdims=True)
    xc = x - mu
    var = jnp.mean(xc * xc, axis=-1, keepdims=True)
    return xc * lax.rsqrt(var + LN_EPS)


def _modulated(x, shift, scale):
    return _layer_norm(x) * (1.0 + scale) + shift


def _sigmoid(x):
    return 1.0 / (1.0 + jnp.exp(-x))


def _mod_kernel(c_ref, w_ref, b_ref, o_ref):
    c = c_ref[...]
    s = (c * _sigmoid(c)).astype(BF16)
    o_ref[0] = _dot(s, w_ref[0].astype(BF16)) + b_ref[0]


def _modulation(cc, w_mod, b_mod):
    L, D, D6 = w_mod.shape
    R = cc.shape[0]
    tn = D
    return pl.pallas_call(
        _mod_kernel,
        out_shape=jax.ShapeDtypeStruct((L, R, D6), F32),
        grid=(L, D6 // tn),
        in_specs=[pl.BlockSpec((R, D), lambda l, j: (0, 0)),
                  pl.BlockSpec((1, D, tn), lambda l, j: (l, 0, j)),
                  pl.BlockSpec((1, 1, tn), lambda l, j: (l, 0, j))],
        out_specs=pl.BlockSpec((1, R, tn), lambda l, j: (l, 0, j)),
        compiler_params=_cparams(2),
        name="mod",
    )(cc, w_mod, b_mod.reshape(L, 1, D6))


def _swap16(x):
    lane = lax.broadcasted_iota(jnp.int32, x.shape, 1)
    up = pltpu.roll(x, LANES - ROPE_PAIRS, 1)
    dn = pltpu.roll(x, ROPE_PAIRS, 1)
    return jnp.where((lane & ROPE_PAIRS) == 0, up, dn)


def _rope(x, cos, sin):
    outs = []
    for j in range(x.shape[1] // LANES):
        xs = x[:, j * LANES:(j + 1) * LANES]
        outs.append(xs * cos + _swap16(xs) * sin)
    return outs[0] if len(outs) == 1 else jnp.concatenate(outs, axis=1)


def _chunk_scan(v, reverse):
    lane = lax.broadcasted_iota(jnp.int32, v.shape, 1)
    sh = 1
    while sh < LANES:
        if reverse:
            v = v + jnp.where(lane < LANES - sh, pltpu.roll(v, LANES - sh, 1), 0.0)
        else:
            v = v + jnp.where(lane >= sh, pltpu.roll(v, sh, 1), 0.0)
        sh *= 2
    return v


def _proj_kernel(x_ref, sh_ref, sc_ref, cos_ref, sin_ref, w_ref, wg_ref, bg_ref,
                 a_ref, qm_ref, km_ref, vm_ref, om_ref, qa_ref, ka_ref, va_ref, g_ref):
    tm = x_ref.shape[1]
    h = _modulated(x_ref[0], sh_ref[0], sc_ref[0]).astype(BF16)

    def cols(j, n=BW):
        return _dot(h, w_ref[:, j:j + n])

    a_ref[0] = cols(0).astype(a_ref.dtype)
    qm_ref[0] = cols(BW).astype(qm_ref.dtype)
    km_ref[0] = cols(2 * BW).astype(km_ref.dtype)
    vm_ref[0] = cols(3 * BW).astype(vm_ref.dtype)
    om_ref[0] = cols(4 * BW).astype(om_ref.dtype)
    cos = cos_ref[...]
    sin = sin_ref[...]
    qa = _rope(cols(5 * BW), cos, sin) * (A_HDIM ** -0.5)
    qa_ref[0] = qa.astype(qa_ref.dtype)
    kv = cols(6 * BW)
    ka_ref[0] = _rope(kv[:, :2 * LANES], cos, sin).astype(ka_ref.dtype)
    va_ref[0] = kv[:, 2 * LANES:].astype(va_ref.dtype)

    z = _dot_nt(wg_ref[...], h) + bg_ref[...]
    row = lax.broadcasted_iota(jnp.int32, (16, LANES), 0)
    is_f = (row & 4) != 0
    is_bwd = row >= 8
    for c in range(tm // LANES):
        zc = z[:, c * LANES:(c + 1) * LANES]
        lf = jnp.minimum(zc, 0.0) - jnp.log1p(jnp.exp(-jnp.abs(zc)))
        cum = jnp.where(is_bwd, _chunk_scan(lf, True), _chunk_scan(lf, False))
        out = jnp.where(is_f, cum, zc)
        g_ref[0, 0, :, c * LANES:(c + 1) * LANES] = out[:8]
        g_ref[0, 1, :, c * LANES:(c + 1) * LANES] = out[8:]


def _project(x, shift, scale, cos, sin, w, wg, bg, tm):
    B, T, D = x.shape
    nw = w.shape[1]
    per_b = shift.shape[0] > 1
    mod_map = (lambda b, i: (b, 0, 0)) if per_b else (lambda b, i: (0, 0, 0))
    tok = lambda n: pl.BlockSpec((1, tm, n), lambda b, i: (b, i, 0))
    outs = [jax.ShapeDtypeStruct((B, T, BW), BF16)] * 6 + [
        jax.ShapeDtypeStruct((B, T, 2 * LANES), BF16),
        jax.ShapeDtypeStruct((B, T, 2 * LANES), BF16),
        jax.ShapeDtypeStruct((B, 2, 8, T), F32)]
    return pl.pallas_call(
        _proj_kernel,
        out_shape=outs,
        grid=(B, T // tm),
        in_specs=[tok(D),
                  pl.BlockSpec((1, 1, D), mod_map),
                  pl.BlockSpec((1, 1, D), mod_map),
                  pl.BlockSpec((tm, LANES), lambda b, i: (i, 0)),
                  pl.BlockSpec((tm, LANES), lambda b, i: (i, 0)),
                  pl.BlockSpec((D, nw), lambda b, i: (0, 0)),
                  pl.BlockSpec((16, D), lambda b, i: (0, 0)),
                  pl.BlockSpec((16, 1), lambda b, i: (0, 0))],
        out_specs=[tok(BW)] * 6 + [tok(2 * LANES), tok(2 * LANES),
                                   pl.BlockSpec((1, 2, 8, tm), lambda b, i: (b, 0, 0, i))],
        compiler_params=_cparams(2),
        name="proj",
    )(x, shift, scale, cos, sin, w, wg, bg)


def _mlstm_kernel(q_ref, k_ref, v_ref, g_ref, c0_ref, m0_ref, h_ref, cn_ref, mn_ref, c_s, m_s):
    d = pl.program_id(1)
    s = pl.program_id(2)
    nc = pl.num_programs(2)
    L = M_CHUNK

    @pl.when(s == 0)
    def _():
        c_s[...] = c0_ref[0, 0]
        m_s[...] = m0_ref[0, 0]

    chunk = jnp.where(d == 0, s, nc - 1 - s)
    row0 = pl.multiple_of(chunk * L, L)
    g = g_ref[0, 0]
    gt = jnp.concatenate([g, jnp.zeros((LANES - 8, L), F32)], axis=0).T
    rr = lax.broadcasted_iota(jnp.int32, (L, L), 0)
    cc = lax.broadcasted_iota(jnp.int32, (L, L), 1)
    causal = ((cc - rr) * (1 - 2 * d)) <= 0
    lane = lax.broadcasted_iota(jnp.int32, (L, LANES), 1)
    ones_col = jnp.where(lane == 0, 1.0, 0.0).astype(BF16)
    kscale = M_HDIM ** -0.5

    for hd in range(M_HEADS):
        sl = slice(hd * M_HDIM, (hd + 1) * M_HDIM)
        qh = q_ref[0, :, sl]
        kh = k_ref[0, :, sl]
        vaug = jnp.concatenate([v_ref[0, :, sl], ones_col], axis=1)
        ig_row = g[hd:hd + 1, :]
        b_row = g[4 + hd:5 + hd, :]
        ig_col = gt[:, hd:hd + 1]
        b_col = gt[:, 4 + hd:5 + hd]
        m_prev = m_s[hd:hd + 1, 0:1]
        c_prev = c_s[hd]

        log_d = jnp.where(causal, b_col + (ig_row - b_row), NEG)
        inter = b_col + m_prev
        m_row = jnp.maximum(inter, jnp.max(log_d, axis=-1, keepdims=True))
        p = _dot_nt(qh, kh) * (jnp.exp(log_d - m_row) * kscale)
        w_inter = jnp.exp(inter - m_row)
        intra = _dot(p.astype(BF16), vaug)
        cross = _dot(qh, c_prev.astype(BF16))
        tot = intra + w_inter * cross
        den = jnp.maximum(jnp.abs(tot[:, M_HDIM:M_HDIM + 1]), jnp.exp(-m_row))
        hh = tot[:, :M_HDIM] / den

        @pl.when(d == 0)
        def _():
            h_ref[0, pl.ds(row0, L), sl] = hh

        @pl.when(d == 1)
        def _():
            h_ref[0, pl.ds(row0, L), sl] += hh

        b_last = jnp.where(d == 0, b_row[:, L - 1:L], b_row[:, 0:1])
        g_row = b_last - b_row + ig_row
        m_new = jnp.maximum(b_last + m_prev, jnp.max(g_row, axis=-1, keepdims=True))
        wk = jnp.exp(b_last - b_col + ig_col - m_new) * kscale
        decay = jnp.exp(b_last + m_prev - m_new)
        c_new = decay * c_prev + _dot_tn(kh, (wk * vaug).astype(BF16))
        c_s[hd] = c_new
        m_s[hd:hd + 1, :] = jnp.broadcast_to(m_new, (1, LANES))

    @pl.when(s == nc - 1)
    def _():
        cn_ref[0, 0] = c_s[...]
        mn_ref[0, 0] = m_s[...]


def _mlstm(qm, km, vm, g, c0, m0):
    B, T, _ = qm.shape
    nc = T // M_CHUNK

    def chunk_map(b, d, s):
        return (b, jnp.where(d == 0, s, nc - 1 - s), 0)

    def g_map(b, d, s):
        return (b, d, 0, jnp.where(d == 0, s, nc - 1 - s))

    st_c = pl.BlockSpec((1, 1, M_HEADS, M_HDIM, 2 * M_HDIM), lambda b, d, s: (b, d, 0, 0, 0))
    st_m = pl.BlockSpec((1, 1, 8, LANES), lambda b, d, s: (b, d, 0, 0))
    tokb = pl.BlockSpec((1, M_CHUNK, BW), chunk_map)
    return pl.pallas_call(
        _mlstm_kernel,
        out_shape=[jax.ShapeDtypeStruct((B, T, BW), F32),
                   jax.ShapeDtypeStruct(c0.shape, F32),
                   jax.ShapeDtypeStruct(m0.shape, F32)],
        grid=(B, 2, nc),
        in_specs=[tokb, tokb, tokb,
                  pl.BlockSpec((1, 1, 8, M_CHUNK), g_map),
                  st_c, st_m],
        out_specs=[pl.BlockSpec((1, T, BW), lambda b, d, s: (b, 0, 0)), st_c, st_m],
        scratch_shapes=[pltpu.VMEM((M_HEADS, M_HDIM, 2 * M_HDIM), F32),
                        pltpu.VMEM((8, LANES), F32)],
        compiler_params=_cparams(3),
        name="mlstm",
    )(qm, km, vm, g, c0, m0)


def _attn_kernel(sink_ref, q_ref, kc_ref, vc_ref, *rest, band):
    if band:
        kp_ref, kb_ref, kn_ref, vp_ref, vb_ref, vn_ref, o_ref = rest
    else:
        (o_ref,) = rest
    i = pl.program_id(1)
    nb = pl.num_programs(1)
    Q = A_BLOCK
    lane = lax.broadcasted_iota(jnp.int32, (Q, LANES), 1)
    low = lane < A_HDIM
    if band:
        rr = lax.broadcasted_iota(jnp.int32, (Q, Q), 0)
        cc = lax.broadcasted_iota(jnp.int32, (Q, Q), 1)
        bias = jnp.concatenate([
            jnp.zeros((Q, kc_ref.shape[1]), F32),
            jnp.where((cc >= rr) & (i > 0), 0.0, NEG),
            jnp.zeros((Q, Q), F32),
            jnp.where((cc <= rr) & (i < nb - 1), 0.0, NEG)], axis=1)
        bias = jnp.concatenate([bias] * A_GROUP, axis=0)

    out_cols = []
    for hk in range(A_KV_HEADS):
        ks = slice(hk * LANES, (hk + 1) * LANES)
        if band:
            kcat = jnp.concatenate([kc_ref[0, :, ks], kp_ref[0, :, ks], kb_ref[0, :, ks], kn_ref[0, :, ks]], axis=0)
            vcat = jnp.concatenate([vc_ref[0, :, ks], vp_ref[0, :, ks], vb_ref[0, :, ks], vn_ref[0, :, ks]], axis=0)
        else:
            kcat = kc_ref[0, :, ks]
            vcat = vc_ref[0, :, ks]
        qs = []
        for gq in range(A_GROUP):
            hq = hk * A_GROUP + gq
            qcol = q_ref[0, :, (hq // 2) * LANES:(hq // 2 + 1) * LANES]
            keep = low if hq % 2 == 0 else jnp.logical_not(low)
            qs.append(jnp.where(keep, qcol, jnp.zeros_like(qcol)))
        qstack = jnp.concatenate(qs, axis=0)
        sc = _dot_nt(qstack, kcat)
        if band:
            sc = sc + bias
        sink = jnp.concatenate(
            [jnp.full((Q, 1), sink_ref[hk, gq], F32) for gq in range(A_GROUP)], axis=0)
        m = jnp.maximum(jnp.max(sc, axis=-1, keepdims=True), sink)
        p = jnp.exp(sc - m)
        den = jnp.sum(p, axis=-1, keepdims=True) + jnp.exp(sink - m)
        o = _dot(p.astype(BF16), vcat) / den
        for pair in range(A_GROUP // 2):
            even = o[(2 * pair) * Q:(2 * pair + 1) * Q]
            odd = o[(2 * pair + 1) * Q:(2 * pair + 2) * Q]
            out_cols.append(jnp.where(low, even, odd))
    o_ref[0] = jnp.concatenate(out_cols, axis=1).astype(o_ref.dtype)


def _attention(q, k, v, kc, vc, sink, band):
    B, T, _ = q.shape
    C = kc.shape[1]
    nb = T // A_BLOCK
    kvw = 2 * LANES
    in_specs = [pl.BlockSpec(memory_space=pltpu.SMEM),
                pl.BlockSpec((1, A_BLOCK, BW), lambda b, i: (b, i, 0)),
                pl.BlockSpec((1, C, kvw), lambda b, i: (b, 0, 0)),
                pl.BlockSpec((1, C, kvw), lambda b, i: (b, 0, 0))]
    args = [sink, q, kc, vc]
    if band:
        prev = pl.BlockSpec((1, A_BLOCK, kvw), lambda b, i: (b, jnp.maximum(i - 1, 0), 0))
        cur = pl.BlockSpec((1, A_BLOCK, kvw), lambda b, i: (b, i, 0))
        nxt = pl.BlockSpec((1, A_BLOCK, kvw), lambda b, i: (b, jnp.minimum(i + 1, nb - 1), 0))
        in_specs += [prev, cur, nxt, prev, cur, nxt]
        args += [k, k, k, v, v, v]
    return pl.pallas_call(
        functools.partial(_attn_kernel, band=band),
        out_shape=jax.ShapeDtypeStruct((B, T, BW), BF16),
        grid=(B, nb),
        in_specs=in_specs,
        out_specs=pl.BlockSpec((1, A_BLOCK, BW), lambda b, i: (b, i, 0)),
        compiler_params=_cparams(2),
        name="attn_band" if band else "attn_ctx",
    )(*args)


def _merge_kernel(x_ref, sh_ref, sc_ref, gt_ref, a_ref, ap_ref, an_ref, hm_ref, om_ref, at_ref,
                  wg_ref, pw_ref, ps_ref, nw_ref, wb_ref, wo_ref, lg_ref, lb_ref,
                  o_ref, abuf, *, alpha):
    i = pl.program_id(1)
    nt = pl.num_programs(1)
    tm = x_ref.shape[1]
    D = x_ref.shape[2]
    x = x_ref[0]
    h = _modulated(x, sh_ref[0], sc_ref[0]).astype(BF16)

    abuf[0:HALO, :] = jnp.where(i > 0, ap_ref[0].astype(F32), 0.0)
    abuf[HALO:HALO + tm, :] = a_ref[0].astype(F32)
    abuf[HALO + tm:, :] = jnp.where(i < nt - 1, an_ref[0].astype(F32), 0.0)
    t = i * tm + lax.broadcasted_iota(jnp.int32, (tm, 1), 0)
    t_seq = nt * tm
    pooled = []
    for gi, w in enumerate(POOL_WINDOWS):
        cs = slice(gi * POOL_GDIM, (gi + 1) * POOL_GDIM)
        acc = abuf[HALO - w // 2:HALO - w // 2 + tm, cs]
        for k in range(1, w):
            acc = acc + abuf[HALO - w // 2 + k:HALO - w // 2 + k + tm, cs]
        cnt = (jnp.minimum(t - w // 2 + w, t_seq) - jnp.maximum(t - w // 2, 0)).astype(F32)
        u = acc / cnt - abuf[HALO:HALO + tm, cs]
        pooled.append(_dot(u.astype(BF16), pw_ref[gi]))
    pool = (jnp.concatenate(pooled, axis=1) * ps_ref[...]).astype(BF16)

    heads = [_layer_norm(hm_ref[0, :, hd * M_HDIM:(hd + 1) * M_HDIM]) for hd in range(M_HEADS)]
    ml = (jnp.concatenate(heads, axis=1) * nw_ref[...] * _sigmoid(om_ref[0].astype(F32))).astype(BF16)

    y = None
    for bi, u in enumerate((pool, ml, at_ref[0])):
        gate = _sigmoid(_dot(h, wg_ref[:, bi * D:(bi + 1) * D]))
        yi = gate * _dot(u, wb_ref[bi])
        y = yi if y is None else y + yi
    z = _dot(y.astype(BF16), wo_ref[...])
    o_ref[0] = _layer_norm(alpha * x + gt_ref[0] * z) * lg_ref[...] + lb_ref[...]


def _merge(x, shift, scale, gate, a, hm, om, at, wg, pw, ps, nw, wb, wo, lg, lb, tm, alpha):
    B, T, D = x.shape
    per_b = shift.shape[0] > 1
    mod_map = (lambda b, i: (b, 0, 0)) if per_b else (lambda b, i: (0, 0, 0))
    tok = lambda n: pl.BlockSpec((1, tm, n), lambda b, i: (b, i, 0))
    full = lambda arr: pl.BlockSpec(arr.shape, lambda b, i: (0,) * arr.ndim)
    r = tm // HALO
    nh = T // HALO
    prev = pl.BlockSpec((1, HALO, BW), lambda b, i: (b, jnp.maximum(i * r - 1, 0), 0))
    nxt = pl.BlockSpec((1, HALO, BW), lambda b, i: (b, jnp.minimum((i + 1) * r, nh - 1), 0))
    mod = pl.BlockSpec((1, 1, D), mod_map)
    return pl.pallas_call(
        functools.partial(_merge_kernel, alpha=alpha),
        out_shape=jax.ShapeDtypeStruct((B, T, D), F32),
        grid=(B, T // tm),
        in_specs=[tok(D), mod, mod, mod, tok(BW), prev, nxt, tok(BW), tok(BW), tok(BW),
                  full(wg), full(pw), full(ps), full(nw), full(wb), full(wo), full(lg), full(lb)],
        out_specs=tok(D),
        scratch_shapes=[pltpu.VMEM((tm + 2 * HALO, BW), F32)],
        compiler_params=_cparams(2),
        name="merge",
    )(x, shift, scale, gate, a, a, a, hm, om, at, wg, pw, ps, nw, wb, wo, lg, lb)


def _ffn_kernel(x_ref, sh_ref, sc_ref, gt_ref, wg_ref, wu_ref, wd_ref, lg_ref, lb_ref, o_ref, *, alpha, fc):
    x = x_ref[0]
    h = _modulated(x, sh_ref[0], sc_ref[0]).astype(BF16)
    dff = wg_ref.shape[1]
    acc = None
    for j in range(dff // fc):
        cs = slice(j * fc, (j + 1) * fc)
        gp = _dot(h, wg_ref[:, cs])
        up = _dot(h, wu_ref[:, cs])
        part = _dot((gp * _sigmoid(gp) * up).astype(BF16), wd_ref[cs, :])
        acc = part if acc is None else acc + part
    o_ref[0] = _layer_norm(alpha * x + gt_ref[0] * acc) * lg_ref[...] + lb_ref[...]


def _ffn(x, shift, scale, gate, wg, wu, wd, lg, lb, tm, alpha):
    B, T, D = x.shape
    per_b = shift.shape[0] > 1
    mod_map = (lambda b, i: (b, 0, 0)) if per_b else (lambda b, i: (0, 0, 0))
    tok = pl.BlockSpec((1, tm, D), lambda b, i: (b, i, 0))
    full = lambda arr: pl.BlockSpec(arr.shape, lambda b, i: (0,) * arr.ndim)
    mod = pl.BlockSpec((1, 1, D), mod_map)
    dff = wg.shape[1]
    fc = dff // 2 if (dff // 2) % LANES == 0 else dff
    return pl.pallas_call(
        functools.partial(_ffn_kernel, alpha=alpha, fc=fc),
        out_shape=jax.ShapeDtypeStruct((B, T, D), F32),
        grid=(B, T // tm),
        in_specs=[tok, mod, mod, mod, full(wg), full(wu), full(wd), full(lg), full(lb)],
        out_specs=tok,
        compiler_params=_cparams(2),
        name="ffn",
    )(x, shift, scale, gate, wg, wu, wd, lg, lb)


def _rope_tables(T):
    t = jnp.arange(T)
    row = (t // GRID_W).astype(F32)
    col = (t % GRID_W).astype(F32)
    inv = ROPE_BASE ** (-jnp.arange(ROPE_PAIRS, dtype=F32) / ROPE_PAIRS)
    ang_r = row[:, None] * inv
    ang_c = col[:, None] * inv
    ang = jnp.concatenate([ang_r, ang_r, ang_c, ang_c], axis=-1)
    sign = jnp.concatenate([-jnp.ones(ROPE_PAIRS), jnp.ones(ROPE_PAIRS)] * 2).astype(F32)
    cos = jnp.tile(jnp.cos(ang), (1, 2))
    sin = jnp.tile(jnp.sin(ang) * sign, (1, 2))
    return cos, sin


def _pack_w_in(w_in_l):
    offs = np.cumsum([BW, BW, BW, BW, BW, 2 * M_HEADS, 2 * M_HEADS, BW, 2 * A_HDIM, 2 * A_HDIM])
    a, qm, km, vm, om, ig, fg, qa, ka, va, gates = jnp.split(w_in_l, offs.tolist(), axis=1)
    dup = lambda w: jnp.concatenate([w[:, :A_HDIM], w[:, :A_HDIM], w[:, A_HDIM:], w[:, A_HDIM:]], axis=1)
    w_main = jnp.concatenate([a, qm, km, vm, om, qa, dup(ka), dup(va)], axis=1).astype(BF16)
    h = M_HEADS
    w_gate_t = jnp.concatenate([ig[:, :h], fg[:, :h], ig[:, h:], fg[:, h:]], axis=1).T.astype(BF16)
    return w_main, w_gate_t, gates.astype(BF16)


def kernel(x, c, ctx, c_ctx, w_mod, b_mod, w_in, pool_w, pool_scale, m_b_i, m_b_f, m_norm_w, attn_sink,
           w_branch, w_out, ln1_g, ln1_b, ln2_g, ln2_b, w_ffn_gate, w_ffn_up, w_ffn_down):
    B, T, D = x.shape
    C = ctx.shape[1]
    depth = w_mod.shape[0]
    alpha = (2 * depth) ** 0.25
    tm_x = 512 if T % 512 == 0 else T
    tm_c = C
    h = M_HEADS

    rows = ((B + 1 + 7) // 8) * 8
    cc = jnp.zeros((rows, D), F32).at[:B].set(c).at[B].set(c_ctx)
    mods = _modulation(cc, w_mod, b_mod)
    cos_x, sin_x = _rope_tables(T)
    cos_c, sin_c = jnp.ones((C, LANES), F32), jnp.zeros((C, LANES), F32)
    zero_c = jnp.zeros((B, 2, h, M_HDIM, 2 * M_HDIM), F32)
    zero_m = jnp.zeros((B, 2, 8, LANES), F32)

    for l in range(depth):
        last = l == depth - 1
        mx = [mods[l, :B, j * D:(j + 1) * D].reshape(B, 1, D) for j in range(6)]
        mc = [mods[l, B:B + 1, j * D:(j + 1) * D].reshape(1, 1, D) for j in range(6)]
        w_main, w_gate_t, w_bgate = _pack_w_in(w_in[l])
        bi, bf = m_b_i[l], m_b_f[l]
        bg = jnp.concatenate([bi[0], bf[0], bi[1], bf[1]]).reshape(16, 1)
        row = lambda v: v.reshape(1, -1)

        px = _project(x, mx[0], mx[1], cos_x, sin_x, w_main, w_gate_t, bg, tm_x)
        pc = _project(ctx, mc[0], mc[1], cos_c, sin_c, w_main, w_gate_t, bg, tm_c)
        a_x, qm_x, km_x, vm_x, om_x, qa_x, ka_x, va_x, g_x = px
        a_c, qm_c, km_c, vm_c, om_c, qa_c, ka_c, va_c, g_c = pc

        h_c, st_c, st_m = _mlstm(qm_c, km_c, vm_c, g_c, zero_c, zero_m)
        h_x, _, _ = _mlstm(qm_x, km_x, vm_x, g_x, st_c, st_m)
        at_x = _attention(qa_x, ka_x, va_x, ka_c, va_c, attn_sink[l], True)

        wargs = (w_bgate, pool_w[l].astype(BF16), row(pool_scale[l]), row(m_norm_w[l]),
                 w_branch[l].astype(BF16), w_out[l].astype(BF16), row(ln1_g[l]), row(ln1_b[l]))
        fargs = (w_ffn_gate[l].astype(BF16), w_ffn_up[l].astype(BF16), w_ffn_down[l].astype(BF16),
                 row(ln2_g[l]), row(ln2_b[l]))
        x1 = _merge(x, mx[0], mx[1], mx[2], a_x, h_x, om_x, at_x, *wargs, tm_x, alpha)
        x_new = _ffn(x1, mx[3], mx[4], mx[5], *fargs, tm_x, alpha)
        if not last:
            at_c = _attention(qa_c, ka_c, va_c, ka_c, va_c, attn_sink[l], False)
            c1 = _merge(ctx, mc[0], mc[1], mc[2], a_c, h_c, om_c, at_c, *wargs, tm_c, alpha)
            ctx = _ffn(c1, mc[3], mc[4], mc[5], *fargs, tm_c, alpha)
        x = x_new
    return x
```

```python
import functools

import jax
import jax.numpy as jnp
import numpy as np
from jax import lax
from jax.experimental import pallas as pl
from jax.experimental.pallas import tpu as pltpu

F32 = jnp.float32
BF16 = jnp.bfloat16

GRID_W = 64
N_BRANCH = 3
POOL_WINDOWS = (2, 4, 8, 16)
POOL_GDIM = 128
M_HEADS = 4
M_HDIM = 128
M_CHUNK = 128
A_HDIM = 64
A_Q_HEADS = 8
A_KV_HEADS = 2
A_GROUP = 4
WINDOW = 128
A_BLOCK = 128
ROPE_BASE = 10000.0
ROPE_PAIRS = 16
BW = 512
LN_EPS = 1e-5
NEG = -1e30
LANES = 128
HALO = 16
VMEM_LIMIT_BYTES = 56 * 1024 * 1024


def _cparams(n_axes):
    return pltpu.CompilerParams(dimension_semantics=("arbitrary",) * n_axes,
                                vmem_limit_bytes=VMEM_LIMIT_BYTES)


def _dot(a, b):
    return jnp.dot(a, b, preferred_element_type=F32)


def _dot_nt(a, b):
    return lax.dot_general(a, b, (((1,), (1,)), ((), ())), preferred_element_type=F32)


def _dot_tn(a, b):
    return lax.dot_general(a, b, (((0,), (0,)), ((), ())), preferred_element_type=F32)


def _layer_norm(x):
    mu = jnp.mean(x, axis=-1, keepdims=True)
    xc = x - mu
    var = jnp.mean(xc * xc, axis=-1, keepdims=True)
    return xc * lax.rsqrt(var + LN_EPS)


def _modulated(x, shift, scale):
    return _layer_norm(x) * (1.0 + scale) + shift


def _sigmoid(x):
    return 0.5 * jnp.tanh(0.5 * x) + 0.5


def _mod_kernel(c_ref, w_ref, b_ref, o_ref):
    c = c_ref[...]
    s = (c * _sigmoid(c)).astype(BF16)
    o_ref[0] = _dot(s, w_ref[0].astype(BF16)) + b_ref[0]


def _modulation(cc, w_mod, b_mod):
    L, D, D6 = w_mod.shape
    R = cc.shape[0]
    tn = D
    return pl.pallas_call(
        _mod_kernel,
        out_shape=jax.ShapeDtypeStruct((L, R, D6), F32),
        grid=(L, D6 // tn),
        in_specs=[pl.BlockSpec((R, D), lambda l, j: (0, 0)),
                  pl.BlockSpec((1, D, tn), lambda l, j: (l, 0, j)),
                  pl.BlockSpec((1, 1, tn), lambda l, j: (l, 0, j))],
        out_specs=pl.BlockSpec((1, R, tn), lambda l, j: (l, 0, j)),
        compiler_params=_cparams(2),
        name="mod",
    )(cc, w_mod, b_mod.reshape(L, 1, D6))


def _swap16(x):
    lane = lax.broadcasted_iota(jnp.int32, x.shape, 1)
    up = pltpu.roll(x, LANES - ROPE_PAIRS, 1)
    dn = pltpu.roll(x, ROPE_PAIRS, 1)
    return jnp.where((lane & ROPE_PAIRS) == 0, up, dn)


def _rope(x, cos, sin):
    outs = []
    for j in range(x.shape[1] // LANES):
        xs = x[:, j * LANES:(j + 1) * LANES]
        outs.append(xs * cos + _swap16(xs) * sin)
    return outs[0] if len(outs) == 1 else jnp.concatenate(outs, axis=1)


def _chunk_scan(v, reverse):
    lane = lax.broadcasted_iota(jnp.int32, v.shape, 1)
    sh = 1
    while sh < LANES:
        if reverse:
            v = v + jnp.where(lane < LANES - sh, pltpu.roll(v, LANES - sh, 1), 0.0)
        else:
            v = v + jnp.where(lane >= sh, pltpu.roll(v, sh, 1), 0.0)
        sh *= 2
    return v


def _proj_kernel(x_ref, sh_ref, sc_ref, cos_ref, sin_ref, w_ref, wk_ref, wg_ref, bg_ref,
                 a_ref, qm_ref, kt_ref, vm_ref, om_ref, qa_ref, ka_ref, va_ref, g_ref):
    tm = x_ref.shape[1]
    h = _modulated(x_ref[0], sh_ref[0], sc_ref[0]).astype(BF16)

    def cols(j, n=BW):
        return _dot(h, w_ref[:, j:j + n])

    a_ref[0] = cols(0).astype(a_ref.dtype)
    qm_ref[0] = cols(BW).astype(qm_ref.dtype)
    vm_ref[0] = cols(2 * BW).astype(vm_ref.dtype)
    om_ref[0] = cols(3 * BW).astype(om_ref.dtype)
    kt_ref[0] = _dot_nt(wk_ref[...], h).astype(kt_ref.dtype)
    cos = cos_ref[...]
    sin = sin_ref[...]
    qa = _rope(cols(4 * BW), cos, sin) * (A_HDIM ** -0.5)
    qa_ref[0] = qa.astype(qa_ref.dtype)
    kv = cols(5 * BW)
    ka_ref[0] = _rope(kv[:, :2 * LANES], cos, sin).astype(ka_ref.dtype)
    va_ref[0] = kv[:, 2 * LANES:].astype(va_ref.dtype)

    z = _dot_nt(wg_ref[...], h) + bg_ref[...]
    row = lax.broadcasted_iota(jnp.int32, (16, LANES), 0)
    is_f = (row & 4) != 0
    is_bwd = row >= 8
    for c in range(tm // LANES):
        zc = z[:, c * LANES:(c + 1) * LANES]
        lf = jnp.minimum(zc, 0.0) - jnp.log1p(jnp.exp(-jnp.abs(zc)))
        cum = jnp.where(is_bwd, _chunk_scan(lf, True), _chunk_scan(lf, False))
        out = jnp.where(is_f, cum, zc)
        g_ref[0, 0, :, c * LANES:(c + 1) * LANES] = out[:8]
        g_ref[0, 1, :, c * LANES:(c + 1) * LANES] = out[8:]


def _project(x, shift, scale, cos, sin, w, wk, wg, bg, tm):
    B, T, D = x.shape
    nw = w.shape[1]
    per_b = shift.shape[0] > 1
    mod_map = (lambda b, i: (b, 0, 0)) if per_b else (lambda b, i: (0, 0, 0))
    tok = lambda n: pl.BlockSpec((1, tm, n), lambda b, i: (b, i, 0))
    tshape = lambda n: jax.ShapeDtypeStruct((B, T, n), BF16)
    outs = [tshape(BW), tshape(BW), jax.ShapeDtypeStruct((B, BW, T), BF16), tshape(BW), tshape(BW), tshape(BW),
            tshape(2 * LANES), tshape(2 * LANES), jax.ShapeDtypeStruct((B, 2, 8, T), F32)]
    return pl.pallas_call(
        _proj_kernel,
        out_shape=outs,
        grid=(B, T // tm),
        in_specs=[tok(D),
                  pl.BlockSpec((1, 1, D), mod_map),
                  pl.BlockSpec((1, 1, D), mod_map),
                  pl.BlockSpec((tm, LANES), lambda b, i: (i, 0)),
                  pl.BlockSpec((tm, LANES), lambda b, i: (i, 0)),
                  pl.BlockSpec((D, nw), lambda b, i: (0, 0)),
                  pl.BlockSpec((BW, D), lambda b, i: (0, 0)),
                  pl.BlockSpec((16, D), lambda b, i: (0, 0)),
                  pl.BlockSpec((16, 1), lambda b, i: (0, 0))],
        out_specs=[tok(BW), tok(BW), pl.BlockSpec((1, BW, tm), lambda b, i: (b, 0, i)), tok(BW), tok(BW), tok(BW),
                   tok(2 * LANES), tok(2 * LANES),
                   pl.BlockSpec((1, 2, 8, tm), lambda b, i: (b, 0, 0, i))],
        compiler_params=_cparams(2),
        name="proj",
    )(x, shift, scale, cos, sin, w, wk, wg, bg)


def _split3(x):
    hi = x.astype(BF16).astype(F32)
    r1 = x - hi
    mid = r1.astype(BF16).astype(F32)
    return hi, mid, r1 - mid


def _mlstm_chain(rev, hd, q_ref, kt_ref, v_ref, g, c_s, m_s, h_ref, row0):
    L = M_CHUNK
    dirn = 1 if rev else 0
    sl = slice(hd * M_HDIM, (hd + 1) * M_HDIM)
    kscale = M_HDIM ** -0.5
    qh = q_ref[0, :, sl]
    kt = kt_ref[0, sl, :]
    vaug = jnp.concatenate([v_ref[0, :, sl], jnp.ones((L, LANES), BF16)], axis=1)
    ig_row = g[hd:hd + 1, :]
    b_row = g[4 + hd:5 + hd, :]
    m_prev = m_s[dirn, hd:hd + 1, :]
    c_prev = c_s[dirn, hd]

    rid = lax.broadcasted_iota(jnp.int32, (16, L), 0)
    b3 = _split3(b_row)
    r3 = _split3(ig_row - b_row)
    lhs = jnp.where(rid == 0, b3[0], jnp.where(rid == 1, b3[1], jnp.where(rid == 2, b3[2],
                    jnp.where(rid < 6, 1.0, 0.0))))
    rhs_l = jnp.where(rid < 3, 1.0, jnp.where(rid == 3, r3[0], jnp.where(rid == 4, r3[1],
                      jnp.where(rid == 5, r3[2], 0.0))))
    rhs_r = jnp.where(rid < 3, 1.0, 0.0)
    db = _dot_tn(lhs.astype(BF16), jnp.concatenate([rhs_l, rhs_r], axis=1).astype(BF16))
    rr = lax.broadcasted_iota(jnp.int32, (L, L), 0)
    cc = lax.broadcasted_iota(jnp.int32, (L, L), 1)
    causal = (cc >= rr) if rev else (cc <= rr)
    log_d = jnp.where(causal, db[:, :L], NEG)
    inter = db[:, L:] + m_prev
    m_row = jnp.maximum(inter, jnp.max(log_d, axis=-1, keepdims=True))
    p = _dot(qh, kt) * (jnp.exp(log_d - m_row) * kscale)
    w_inter = jnp.exp(inter - m_row)
    intra = _dot(p.astype(BF16), vaug)
    cross = _dot(qh, c_prev.astype(BF16))
    num = intra[:, :M_HDIM] + w_inter * cross[:, :M_HDIM]
    nq = intra[:, M_HDIM:] + w_inter * cross[:, M_HDIM:]
    den = jnp.maximum(jnp.abs(nq), jnp.exp(-m_row))
    h_ref[0, pl.ds(row0, L), sl] += num / den

    b_last = b_row[:, 0:1] if rev else b_row[:, L - 1:L]
    g_row = b_last - b_row + ig_row
    m_new = jnp.maximum(b_last + m_prev, jnp.max(g_row, axis=-1, keepdims=True))
    wk = jnp.exp(g_row - m_new) * kscale
    decay = jnp.exp(b_last + m_prev - m_new)
    upd = _dot((kt.astype(F32) * wk).astype(BF16), vaug)
    c_s[dirn, hd] = jnp.concatenate([decay, decay], axis=1) * c_prev + upd
    m_s[dirn, hd:hd + 1, :] = m_new


def _mlstm_kernel(qf_ref, qb_ref, ktf_ref, ktb_ref, vf_ref, vb_ref, gf_ref, gb_ref, c0_ref, m0_ref,
                  h_ref, cn_ref, mn_ref, c_s, m_s):
    s = pl.program_id(1)
    nc = pl.num_programs(1)
    L = M_CHUNK

    @pl.when(s == 0)
    def _():
        c_s[...] = c0_ref[0]
        m_s[...] = m0_ref[0]
        h_ref[...] = jnp.zeros_like(h_ref)

    row_f = pl.multiple_of(s * L, L)
    row_b = pl.multiple_of((nc - 1 - s) * L, L)
    gf = gf_ref[0, 0]
    gb = gb_ref[0, 0]
    for hd in range(M_HEADS):
        _mlstm_chain(False, hd, qf_ref, ktf_ref, vf_ref, gf, c_s, m_s, h_ref, row_f)
        _mlstm_chain(True, hd, qb_ref, ktb_ref, vb_ref, gb, c_s, m_s, h_ref, row_b)

    @pl.when(s == nc - 1)
    def _():
        cn_ref[0] = c_s[...]
        mn_ref[0] = m_s[...]


def _mlstm(qm, kmt, vm, g, c0, m0):
    B, T, _ = qm.shape
    nc = T // M_CHUNK
    st_c = pl.BlockSpec((1, 2, M_HEADS, M_HDIM, 2 * M_HDIM), lambda b, s: (b, 0, 0, 0, 0))
    st_m = pl.BlockSpec((1, 2, 8, LANES), lambda b, s: (b, 0, 0, 0))
    tok_f = pl.BlockSpec((1, M_CHUNK, BW), lambda b, s: (b, s, 0))
    tok_b = pl.BlockSpec((1, M_CHUNK, BW), lambda b, s: (b, nc - 1 - s, 0))
    kt_f = pl.BlockSpec((1, BW, M_CHUNK), lambda b, s: (b, 0, s))
    kt_b = pl.BlockSpec((1, BW, M_CHUNK), lambda b, s: (b, 0, nc - 1 - s))
    g_f = pl.BlockSpec((1, 1, 8, M_CHUNK), lambda b, s: (b, 0, 0, s))
    g_b = pl.BlockSpec((1, 1, 8, M_CHUNK), lambda b, s: (b, 1, 0, nc - 1 - s))
    return pl.pallas_call(
        _mlstm_kernel,
        out_shape=[jax.ShapeDtypeStruct((B, T, BW), F32),
                   jax.ShapeDtypeStruct(c0.shape, F32),
                   jax.ShapeDtypeStruct(m0.shape, F32)],
        grid=(B, nc),
        in_specs=[tok_f, tok_b, kt_f, kt_b, tok_f, tok_b, g_f, g_b, st_c, st_m],
        out_specs=[pl.BlockSpec((1, T, BW), lambda b, s: (b, 0, 0)), st_c, st_m],
        scratch_shapes=[pltpu.VMEM((2, M_HEADS, M_HDIM, 2 * M_HDIM), F32),
                        pltpu.VMEM((2, 8, LANES), F32)],
        compiler_params=_cparams(2),
        name="mlstm",
    )(qm, qm, kmt, kmt, vm, vm, g, g, c0, m0)


def _attn_kernel(sink_ref, q_ref, kc_ref, vc_ref, *rest, band):
    if band:
        kp_ref, kb_ref, kn_ref, vp_ref, vb_ref, vn_ref, o_ref = rest
    else:
        (o_ref,) = rest
    i = pl.program_id(1)
    nb = pl.num_programs(1)
    Q = A_BLOCK
    lane = lax.broadcasted_iota(jnp.int32, (Q, LANES), 1)
    low = lane < A_HDIM
    if band:
        rr = lax.broadcasted_iota(jnp.int32, (Q, Q), 0)
        cc = lax.broadcasted_iota(jnp.int32, (Q, Q), 1)
        bias = jnp.concatenate([
            jnp.zeros((Q, kc_ref.shape[1]), F32),
            jnp.where((cc >= rr) & (i > 0), 0.0, NEG),
            jnp.zeros((Q, Q), F32),
            jnp.where((cc <= rr) & (i < nb - 1), 0.0, NEG)], axis=1)
        bias = jnp.concatenate([bias] * A_GROUP, axis=0)

    out_cols = []
    for hk in range(A_KV_HEADS):
        ks = slice(hk * LANES, (hk + 1) * LANES)
        if band:
            kcat = jnp.concatenate([kc_ref[0, :, ks], kp_ref[0, :, ks], kb_ref[0, :, ks], kn_ref[0, :, ks]], axis=0)
            vcat = jnp.concatenate([vc_ref[0, :, ks], vp_ref[0, :, ks], vb_ref[0, :, ks], vn_ref[0, :, ks]], axis=0)
        else:
            kcat = kc_ref[0, :, ks]
            vcat = vc_ref[0, :, ks]
        qs = []
        for gq in range(A_GROUP):
            hq = hk * A_GROUP + gq
            qcol = q_ref[0, :, (hq // 2) * LANES:(hq // 2 + 1) * LANES]
            keep = low if hq % 2 == 0 else jnp.logical_not(low)
            qs.append(jnp.where(keep, qcol, jnp.zeros_like(qcol)))
        qstack = jnp.concatenate(qs, axis=0)
        sc = _dot_nt(qstack, kcat)
        if band:
            sc = sc + bias
        sink = jnp.concatenate(
            [jnp.full((Q, 1), sink_ref[hk, gq], F32) for gq in range(A_GROUP)], axis=0)
        m = jnp.maximum(jnp.max(sc, axis=-1, keepdims=True), sink)
        p = jnp.exp(sc - m)
        den = jnp.sum(p, axis=-1, keepdims=True) + jnp.exp(sink - m)
        o = _dot(p.astype(BF16), vcat) / den
        for pair in range(A_GROUP // 2):
            even = o[(2 * pair) * Q:(2 * pair + 1) * Q]
            odd = o[(2 * pair + 1) * Q:(2 * pair + 2) * Q]
            out_cols.append(jnp.where(low, even, odd))
    o_ref[0] = jnp.concatenate(out_cols, axis=1).astype(o_ref.dtype)


def _attention(q, k, v, kc, vc, sink, band):
    B, T, _ = q.shape
    C = kc.shape[1]
    nb = T // A_BLOCK
    kvw = 2 * LANES
    in_specs = [pl.BlockSpec(memory_space=pltpu.SMEM),
                pl.BlockSpec((1, A_BLOCK, BW), lambda b, i: (b, i, 0)),
                pl.BlockSpec((1, C, kvw), lambda b, i: (b, 0, 0)),
                pl.BlockSpec((1, C, kvw), lambda b, i: (b, 0, 0))]
    args = [sink, q, kc, vc]
    if band:
        prev = pl.BlockSpec((1, A_BLOCK, kvw), lambda b, i: (b, jnp.maximum(i - 1, 0), 0))
        cur = pl.BlockSpec((1, A_BLOCK, kvw), lambda b, i: (b, i, 0))
        nxt = pl.BlockSpec((1, A_BLOCK, kvw), lambda b, i: (b, jnp.minimum(i + 1, nb - 1), 0))
        in_specs += [prev, cur, nxt, prev, cur, nxt]
        args += [k, k, k, v, v, v]
    return pl.pallas_call(
        functools.partial(_attn_kernel, band=band),
        out_shape=jax.ShapeDtypeStruct((B, T, BW), BF16),
        grid=(B, nb),
        in_specs=in_specs,
        out_specs=pl.BlockSpec((1, A_BLOCK, BW), lambda b, i: (b, i, 0)),
        compiler_params=_cparams(2),
        name="attn_band" if band else "attn_ctx",
    )(*args)


def _merge_kernel(x_ref, sh_ref, sc_ref, gt_ref, a_ref, ap_ref, an_ref, hm_ref, om_ref, at_ref,
                  wg_ref, pw_ref, ps_ref, nw_ref, wb_ref, wo_ref, lg_ref, lb_ref,
                  o_ref, abuf, *, alpha):
    i = pl.program_id(1)
    nt = pl.num_programs(1)
    tm = x_ref.shape[1]
    D = x_ref.shape[2]
    x = x_ref[0]
    h = _modulated(x, sh_ref[0], sc_ref[0]).astype(BF16)

    abuf[0:HALO, :] = jnp.where(i > 0, ap_ref[0].astype(F32), 0.0)
    abuf[HALO:HALO + tm, :] = a_ref[0].astype(F32)
    abuf[HALO + tm:, :] = jnp.where(i < nt - 1, an_ref[0].astype(F32), 0.0)
    t = i * tm + lax.broadcasted_iota(jnp.int32, (tm, 1), 0)
    t_seq = nt * tm
    pooled = []
    for gi, w in enumerate(POOL_WINDOWS):
        cs = slice(gi * POOL_GDIM, (gi + 1) * POOL_GDIM)
        acc = abuf[HALO - w // 2:HALO - w // 2 + tm, cs]
        for k in range(1, w):
            acc = acc + abuf[HALO - w // 2 + k:HALO - w // 2 + k + tm, cs]
        cnt = (jnp.minimum(t - w // 2 + w, t_seq) - jnp.maximum(t - w // 2, 0)).astype(F32)
        u = acc / cnt - abuf[HALO:HALO + tm, cs]
        pooled.append(_dot(u.astype(BF16), pw_ref[gi]))
    pool = (jnp.concatenate(pooled, axis=1) * ps_ref[...]).astype(BF16)

    heads = [_layer_norm(hm_ref[0, :, hd * M_HDIM:(hd + 1) * M_HDIM]) for hd in range(M_HEADS)]
    ml = (jnp.concatenate(heads, axis=1) * nw_ref[...] * _sigmoid(om_ref[0].astype(F32))).astype(BF16)

    y = None
    for bi, u in enumerate((pool, ml, at_ref[0])):
        gate = _sigmoid(_dot(h, wg_ref[:, bi * D:(bi + 1) * D]))
        yi = gate * _dot(u, wb_ref[bi])
        y = yi if y is None else y + yi
    z = _dot(y.astype(BF16), wo_ref[...])
    o_ref[0] = _layer_norm(alpha * x + gt_ref[0] * z) * lg_ref[...] + lb_ref[...]


def _merge(x, shift, scale, gate, a, hm, om, at, wg, pw, ps, nw, wb, wo, lg, lb, tm, alpha):
    B, T, D = x.shape
    per_b = shift.shape[0] > 1
    mod_map = (lambda b, i: (b, 0, 0)) if per_b else (lambda b, i: (0, 0, 0))
    tok = lambda n: pl.BlockSpec((1, tm, n), lambda b, i: (b, i, 0))
    full = lambda arr: pl.BlockSpec(arr.shape, lambda b, i: (0,) * arr.ndim)
    r = tm // HALO
    nh = T // HALO
    prev = pl.BlockSpec((1, HALO, BW), lambda b, i: (b, jnp.maximum(i * r - 1, 0), 0))
    nxt = pl.BlockSpec((1, HALO, BW), lambda b, i: (b, jnp.minimum((i + 1) * r, nh - 1), 0))
    mod = pl.BlockSpec((1, 1, D), mod_map)
    return pl.pallas_call(
        functools.partial(_merge_kernel, alpha=alpha),
        out_shape=jax.ShapeDtypeStruct((B, T, D), F32),
        grid=(B, T // tm),
        in_specs=[tok(D), mod, mod, mod, tok(BW), prev, nxt, tok(BW), tok(BW), tok(BW),
                  full(wg), full(pw), full(ps), full(nw), full(wb), full(wo), full(lg), full(lb)],
        out_specs=tok(D),
        scratch_shapes=[pltpu.VMEM((tm + 2 * HALO, BW), F32)],
        compiler_params=_cparams(2),
        name="merge",
    )(x, shift, scale, gate, a, a, a, hm, om, at, wg, pw, ps, nw, wb, wo, lg, lb)


def _ffn_kernel(x_ref, sh_ref, sc_ref, gt_ref, wg_ref, wu_ref, wd_ref, lg_ref, lb_ref, o_ref, *, alpha, fc):
    x = x_ref[0]
    h = _modulated(x, sh_ref[0], sc_ref[0]).astype(BF16)
    dff = wg_ref.shape[1]
    acc = None
    for j in range(dff // fc):
        cs = slice(j * fc, (j + 1) * fc)
        gp = _dot(h, wg_ref[:, cs])
        up = _dot(h, wu_ref[:, cs])
        part = _dot((gp * _sigmoid(gp) * up).astype(BF16), wd_ref[cs, :])
        acc = part if acc is None else acc + part
    o_ref[0] = _layer_norm(alpha * x + gt_ref[0] * acc) * lg_ref[...] + lb_ref[...]


def _ffn(x, shift, scale, gate, wg, wu, wd, lg, lb, tm, alpha):
    B, T, D = x.shape
    per_b = shift.shape[0] > 1
    mod_map = (lambda b, i: (b, 0, 0)) if per_b else (lambda b, i: (0, 0, 0))
    tok = pl.BlockSpec((1, tm, D), lambda b, i: (b, i, 0))
    full = lambda arr: pl.BlockSpec(arr.shape, lambda b, i: (0,) * arr.ndim)
    mod = pl.BlockSpec((1, 1, D), mod_map)
    dff = wg.shape[1]
    fc = dff // 2 if (dff // 2) % LANES == 0 else dff
    return pl.pallas_call(
        functools.partial(_ffn_kernel, alpha=alpha, fc=fc),
        out_shape=jax.ShapeDtypeStruct((B, T, D), F32),
        grid=(B, T // tm),
        in_specs=[tok, mod, mod, mod, full(wg), full(wu), full(wd), full(lg), full(lb)],
        out_specs=tok,
        compiler_params=_cparams(2),
        name="ffn",
    )(x, shift, scale, gate, wg, wu, wd, lg, lb)


def _rope_tables(T):
    t = jnp.arange(T)
    row = (t // GRID_W).astype(F32)
    col = (t % GRID_W).astype(F32)
    inv = ROPE_BASE ** (-jnp.arange(ROPE_PAIRS, dtype=F32) / ROPE_PAIRS)
    ang_r = row[:, None] * inv
    ang_c = col[:, None] * inv
    ang = jnp.concatenate([ang_r, ang_r, ang_c, ang_c], axis=-1)
    sign = jnp.concatenate([-jnp.ones(ROPE_PAIRS), jnp.ones(ROPE_PAIRS)] * 2).astype(F32)
    cos = jnp.tile(jnp.cos(ang), (1, 2))
    sin = jnp.tile(jnp.sin(ang) * sign, (1, 2))
    return cos, sin


def _pack_w_in(w_in_l):
    offs = np.cumsum([BW, BW, BW, BW, BW, 2 * M_HEADS, 2 * M_HEADS, BW, 2 * A_HDIM, 2 * A_HDIM])
    a, qm, km, vm, om, ig, fg, qa, ka, va, gates = jnp.split(w_in_l, offs.tolist(), axis=1)
    dup = lambda w: jnp.concatenate([w[:, :A_HDIM], w[:, :A_HDIM], w[:, A_HDIM:], w[:, A_HDIM:]], axis=1)
    w_main = jnp.concatenate([a, qm, vm, om, qa, dup(ka), dup(va)], axis=1).astype(BF16)
    h = M_HEADS
    w_gate_t = jnp.concatenate([ig[:, :h], fg[:, :h], ig[:, h:], fg[:, h:]], axis=1).T.astype(BF16)
    return w_main, km.T.astype(BF16), w_gate_t, gates.astype(BF16)


def kernel(x, c, ctx, c_ctx, w_mod, b_mod, w_in, pool_w, pool_scale, m_b_i, m_b_f, m_norm_w, attn_sink,
           w_branch, w_out, ln1_g, ln1_b, ln2_g, ln2_b, w_ffn_gate, w_ffn_up, w_ffn_down):
    B, T, D = x.shape
    C = ctx.shape[1]
    depth = w_mod.shape[0]
    alpha = (2 * depth) ** 0.25
    tm_x = 512 if T % 512 == 0 else T
    tm_c = C
    h = M_HEADS

    rows = ((B + 1 + 7) // 8) * 8
    cc = jnp.zeros((rows, D), F32).at[:B].set(c).at[B].set(c_ctx)
    mods = _modulation(cc, w_mod, b_mod)
    cos_x, sin_x = _rope_tables(T)
    cos_c, sin_c = jnp.ones((C, LANES), F32), jnp.zeros((C, LANES), F32)
    zero_c = jnp.zeros((B, 2, h, M_HDIM, 2 * M_HDIM), F32)
    zero_m = jnp.zeros((B, 2, 8, LANES), F32)

    for l in range(depth):
        last = l == depth - 1
        mx = [mods[l, :B, j * D:(j + 1) * D].reshape(B, 1, D) for j in range(6)]
        mc = [mods[l, B:B + 1, j * D:(j + 1) * D].reshape(1, 1, D) for j in range(6)]
        w_main, w_kt, w_gate_t, w_bgate = _pack_w_in(w_in[l])
        bi, bf = m_b_i[l], m_b_f[l]
        bg = jnp.concatenate([bi[0], bf[0], bi[1], bf[1]]).reshape(16, 1)
        row = lambda v: v.reshape(1, -1)

        px = _project(x, mx[0], mx[1], cos_x, sin_x, w_main, w_kt, w_gate_t, bg, tm_x)
        pc = _project(ctx, mc[0], mc[1], cos_c, sin_c, w_main, w_kt, w_gate_t, bg, tm_c)
        a_x, qm_x, kt_x, vm_x, om_x, qa_x, ka_x, va_x, g_x = px
        a_c, qm_c, kt_c, vm_c, om_c, qa_c, ka_c, va_c, g_c = pc

        h_c, st_c, st_m = _mlstm(qm_c, kt_c, vm_c, g_c, zero_c, zero_m)
        h_x, _, _ = _mlstm(qm_x, kt_x, vm_x, g_x, st_c, st_m)
        at_x = _attention(qa_x, ka_x, va_x, ka_c, va_c, attn_sink[l], True)

        wargs = (w_bgate, pool_w[l].astype(BF16), row(pool_scale[l]), row(m_norm_w[l]),
                 w_branch[l].astype(BF16), w_out[l].astype(BF16), row(ln1_g[l]), row(ln1_b[l]))
        fargs = (w_ffn_gate[l].astype(BF16), w_ffn_up[l].astype(BF16), w_ffn_down[l].astype(BF16),
                 row(ln2_g[l]), row(ln2_b[l]))
        x1 = _merge(x, mx[0], mx[1], mx[2], a_x, h_x, om_x, at_x, *wargs, tm_x, alpha)
        x_new = _ffn(x1, mx[3], mx[4], mx[5], *fargs, tm_x, alpha)
        if not last:
            at_c = _attention(qa_c, ka_c, va_c, ka_c, va_c, attn_sink[l], False)
            c1 = _merge(ctx, mc[0], mc[1], mc[2], a_c, h_c, om_c, at_c, *wargs, tm_c, alpha)
            ctx = _ffn(c1, mc[3], mc[4], mc[5], *fargs, tm_c, alpha)
        x = x_new
    return x
```

```python
import functools

import jax
import jax.numpy as jnp
import numpy as np
from jax import lax
from jax.experimental import pallas as pl
from jax.experimental.pallas import tpu as pltpu

F32 = jnp.float32
BF16 = jnp.bfloat16

GRID_W = 64
N_BRANCH = 3
POOL_WINDOWS = (2, 4, 8, 16)
POOL_GDIM = 128
M_HEADS = 4
M_HDIM = 128
M_CHUNK = 128
A_HDIM = 64
A_Q_HEADS = 8
A_KV_HEADS = 2
A_GROUP = 4
WINDOW = 128
A_BLOCK = 128
ROPE_BASE = 10000.0
ROPE_PAIRS = 16
BW = 512
LN_EPS = 1e-5
NEG = -1e30
LOG2E = 1.4426950408889634
LANES = 128
HALO = 16
PROJ_SLAB = 512
VMEM_LIMIT_BYTES = 56 * 1024 * 1024


def _cparams(n_axes):
    return pltpu.CompilerParams(dimension_semantics=("arbitrary",) * n_axes,
                                vmem_limit_bytes=VMEM_LIMIT_BYTES)


def _dot(a, b):
    return jnp.dot(a, b, preferred_element_type=F32)


def _dot_nt(a, b):
    return lax.dot_general(a, b, (((1,), (1,)), ((), ())), preferred_element_type=F32)


def _dot_tn(a, b):
    return lax.dot_general(a, b, (((0,), (0,)), ((), ())), preferred_element_type=F32)


def _layer_norm(x):
    mu = jnp.mean(x, axis=-1, keepdims=True)
    xc = x - mu
    var = jnp.mean(xc * xc, axis=-1, keepdims=True)
    return xc * lax.rsqrt(var + LN_EPS)


def _modulated(x, shift, scale):
    return _layer_norm(x) * (1.0 + scale) + shift


def _sigmoid(x):
    return 0.5 * jnp.tanh(0.5 * x) + 0.5


def _mod_kernel(c_ref, w_ref, b_ref, o_ref):
    c = c_ref[...]
    s = (c * _sigmoid(c)).astype(BF16)
    o_ref[0] = _dot(s, w_ref[0].astype(BF16)) + b_ref[0]


def _modulation(cc, w_mod, b_mod):
    L, D, D6 = w_mod.shape
    R = cc.shape[0]
    tn = D
    return pl.pallas_call(
        _mod_kernel,
        out_shape=jax.ShapeDtypeStruct((L, R, D6), F32),
        grid=(L, D6 // tn),
        in_specs=[pl.BlockSpec((R, D), lambda l, j: (0, 0)),
                  pl.BlockSpec((1, D, tn), lambda l, j: (l, 0, j)),
                  pl.BlockSpec((1, 1, tn), lambda l, j: (l, 0, j))],
        out_specs=pl.BlockSpec((1, R, tn), lambda l, j: (l, 0, j)),
        compiler_params=_cparams(2),
        name="mod",
    )(cc, w_mod, b_mod.reshape(L, 1, D6))


def _swap16(x):
    lane = lax.broadcasted_iota(jnp.int32, x.shape, 1)
    up = pltpu.roll(x, LANES - ROPE_PAIRS, 1)
    dn = pltpu.roll(x, ROPE_PAIRS, 1)
    return jnp.where((lane & ROPE_PAIRS) == 0, up, dn)


def _rope(x, cos, sin):
    outs = []
    for j in range(x.shape[1] // LANES):
        xs = x[:, j * LANES:(j + 1) * LANES]
        outs.append(xs * cos + _swap16(xs) * sin)
    return outs[0] if len(outs) == 1 else jnp.concatenate(outs, axis=1)


def _chunk_scan(v, reverse):
    lane = lax.broadcasted_iota(jnp.int32, v.shape, 1)
    sh = 1
    while sh < LANES:
        if reverse:
            v = v + jnp.where(lane < LANES - sh, pltpu.roll(v, LANES - sh, 1), 0.0)
        else:
            v = v + jnp.where(lane >= sh, pltpu.roll(v, sh, 1), 0.0)
        sh *= 2
    return v


def _proj_kernel(x_ref, sh_ref, sc_ref, cos_ref, sin_ref, w_ref, wk_ref, wg_ref, bg_ref,
                 a_ref, qm_ref, kt_ref, vm_ref, om_ref, qa_ref, ka_ref, va_ref, g_ref):
    tm = x_ref.shape[1]
    rows = min(tm, PROJ_SLAB)

    def slab(r0):
        rs = slice(r0, r0 + rows)
        h = _modulated(x_ref[0, rs, :], sh_ref[0], sc_ref[0]).astype(BF16)
        yield

        def cols(j, n=BW):
            return _dot(h, w_ref[:, j:j + n])

        a_ref[0, rs, :] = cols(0).astype(a_ref.dtype)
        qm_ref[0, rs, :] = cols(BW).astype(qm_ref.dtype)
        vm_ref[0, rs, :] = cols(2 * BW).astype(vm_ref.dtype)
        om_ref[0, rs, :] = cols(3 * BW).astype(om_ref.dtype)
        kt_ref[0, :, rs] = _dot_nt(wk_ref[...], h).astype(kt_ref.dtype)
        qa = cols(4 * BW)
        kv = cols(5 * BW)
        z = _dot_nt(wg_ref[...], h) + bg_ref[...]
        yield
        cos = cos_ref[rs, :]
        sin = sin_ref[rs, :]
        qa = _rope(qa, cos, sin) * (A_HDIM ** -0.5 * LOG2E)
        qa_ref[0, rs, :] = qa.astype(qa_ref.dtype)
        ka_ref[0, rs, :] = _rope(kv[:, :2 * LANES], cos, sin).astype(ka_ref.dtype)
        va_ref[0, rs, :] = kv[:, 2 * LANES:].astype(va_ref.dtype)
        row = lax.broadcasted_iota(jnp.int32, (16, LANES), 0)
        is_f = (row & 4) != 0
        is_bwd = row >= 8
        for c in range(rows // LANES):
            zc = z[:, c * LANES:(c + 1) * LANES]
            lf = jnp.minimum(zc, 0.0) - jnp.log1p(jnp.exp(-jnp.abs(zc)))
            cum = jnp.where(is_bwd, _chunk_scan(lf, True), _chunk_scan(lf, False))
            out = jnp.where(is_f, cum, zc)
            g_ref[0, 0, :, r0 + c * LANES:r0 + (c + 1) * LANES] = out[:8]
            g_ref[0, 1, :, r0 + c * LANES:r0 + (c + 1) * LANES] = out[8:]
        yield

    slabs = [slab(r0) for r0 in range(0, tm, rows)]
    for _ in range(3):
        for sb in slabs:
            next(sb)


def _project(x, shift, scale, cos, sin, w, wk, wg, bg, tm):
    B, T, D = x.shape
    nw = w.shape[1]
    per_b = shift.shape[0] > 1
    mod_map = (lambda b, i: (b, 0, 0)) if per_b else (lambda b, i: (0, 0, 0))
    tok = lambda n: pl.BlockSpec((1, tm, n), lambda b, i: (b, i, 0))
    tshape = lambda n: jax.ShapeDtypeStruct((B, T, n), BF16)
    outs = [tshape(BW), tshape(BW), jax.ShapeDtypeStruct((B, BW, T), BF16), tshape(BW), tshape(BW), tshape(BW),
            tshape(2 * LANES), tshape(2 * LANES), jax.ShapeDtypeStruct((B, 2, 8, T), F32)]
    return pl.pallas_call(
        _proj_kernel,
        out_shape=outs,
        grid=(B, T // tm),
        in_specs=[tok(D),
                  pl.BlockSpec((1, 1, D), mod_map),
                  pl.BlockSpec((1, 1, D), mod_map),
                  pl.BlockSpec((tm, LANES), lambda b, i: (i, 0)),
                  pl.BlockSpec((tm, LANES), lambda b, i: (i, 0)),
                  pl.BlockSpec((D, nw), lambda b, i: (0, 0)),
                  pl.BlockSpec((BW, D), lambda b, i: (0, 0)),
                  pl.BlockSpec((16, D), lambda b, i: (0, 0)),
                  pl.BlockSpec((16, 1), lambda b, i: (0, 0))],
        out_specs=[tok(BW), tok(BW), pl.BlockSpec((1, BW, tm), lambda b, i: (b, 0, i)), tok(BW), tok(BW), tok(BW),
                   tok(2 * LANES), tok(2 * LANES),
                   pl.BlockSpec((1, 2, 8, tm), lambda b, i: (b, 0, 0, i))],
        compiler_params=_cparams(2),
        name="proj",
    )(x, shift, scale, cos, sin, w, wk, wg, bg)


def _split3(x):
    hi = x.astype(BF16).astype(F32)
    r1 = x - hi
    mid = r1.astype(BF16).astype(F32)
    return hi, mid, r1 - mid


def _mlstm_chain(rev, hd, q_ref, kt_ref, v_ref, g, c_s, m_s, h_ref, row0):
    L = M_CHUNK
    dirn = 1 if rev else 0
    sl = slice(hd * M_HDIM, (hd + 1) * M_HDIM)
    kscale = M_HDIM ** -0.5
    qh = q_ref[0, :, sl]
    kt = kt_ref[0, sl, :]
    vaug = jnp.concatenate([v_ref[0, :, sl], jnp.ones((L, LANES), BF16)], axis=1)
    ig_row = g[hd:hd + 1, :]
    b_row = g[4 + hd:5 + hd, :]
    m_prev = m_s[dirn, hd:hd + 1, :]
    c_prev = c_s[dirn, hd]

    rid = lax.broadcasted_iota(jnp.int32, (16, L), 0)
    b3 = _split3(b_row)
    r3 = _split3(ig_row - b_row)
    lhs = jnp.where(rid == 0, b3[0], jnp.where(rid == 1, b3[1], jnp.where(rid == 2, b3[2],
                    jnp.where(rid < 6, 1.0, 0.0))))
    rhs_l = jnp.where(rid < 3, 1.0, jnp.where(rid == 3, r3[0], jnp.where(rid == 4, r3[1],
                      jnp.where(rid == 5, r3[2], 0.0))))
    rhs_r = jnp.where(rid < 3, 1.0, 0.0)
    db = _dot_tn(lhs.astype(BF16), jnp.concatenate([rhs_l, rhs_r], axis=1).astype(BF16))
    qk = _dot(qh, kt)
    cross = _dot(qh, c_prev.astype(BF16))
    yield
    rr = lax.broadcasted_iota(jnp.int32, (L, L), 0)
    cc = lax.broadcasted_iota(jnp.int32, (L, L), 1)
    causal = (cc >= rr) if rev else (cc <= rr)
    log_d = jnp.where(causal, db[:, :L], NEG)
    inter = db[:, L:] + m_prev
    m_row = jnp.maximum(inter, jnp.max(log_d, axis=-1, keepdims=True))
    p = (qk * (jnp.exp(log_d - m_row) * kscale)).astype(BF16)
    w_inter = jnp.exp(inter - m_row)
    yield
    intra = _dot(p, vaug)
    b_last = b_row[:, 0:1] if rev else b_row[:, L - 1:L]
    g_row = b_last - b_row + ig_row
    m_new = jnp.maximum(b_last + m_prev, jnp.max(g_row, axis=-1, keepdims=True))
    wk = jnp.exp(g_row - m_new) * kscale
    decay = jnp.exp(b_last + m_prev - m_new)
    upd = _dot((kt.astype(F32) * wk).astype(BF16), vaug)
    yield
    num = intra[:, :M_HDIM] + w_inter * cross[:, :M_HDIM]
    nq = intra[:, M_HDIM:] + w_inter * cross[:, M_HDIM:]
    den = jnp.maximum(jnp.abs(nq), jnp.exp(-m_row))
    h_ref[0, pl.ds(row0, L), sl] += num / den
    c_s[dirn, hd] = jnp.concatenate([decay, decay], axis=1) * c_prev + upd
    m_s[dirn, hd:hd + 1, :] = m_new
    yield


def _mlstm_kernel(qf_ref, qb_ref, ktf_ref, ktb_ref, vf_ref, vb_ref, gf_ref, gb_ref, c0_ref, m0_ref,
                  h_ref, cn_ref, mn_ref, c_s, m_s):
    s = pl.program_id(1)
    nc = pl.num_programs(1)
    L = M_CHUNK

    @pl.when(s == 0)
    def _():
        c_s[...] = c0_ref[0]
        m_s[...] = m0_ref[0]
        h_ref[...] = jnp.zeros_like(h_ref)

    row_f = pl.multiple_of(s * L, L)
    row_b = pl.multiple_of((nc - 1 - s) * L, L)
    gf = gf_ref[0, 0]
    gb = gb_ref[0, 0]
    chains = []
    for hd in range(M_HEADS):
        chains.append(_mlstm_chain(False, hd, qf_ref, ktf_ref, vf_ref, gf, c_s, m_s, h_ref, row_f))
        chains.append(_mlstm_chain(True, hd, qb_ref, ktb_ref, vb_ref, gb, c_s, m_s, h_ref, row_b))
    for _ in range(4):
        for ch in chains:
            next(ch)

    @pl.when(s == nc - 1)
    def _():
        cn_ref[0] = c_s[...]
        mn_ref[0] = m_s[...]


def _mlstm(qm, kmt, vm, g, c0, m0):
    B, T, _ = qm.shape
    nc = T // M_CHUNK
    st_c = pl.BlockSpec((1, 2, M_HEADS, M_HDIM, 2 * M_HDIM), lambda b, s: (b, 0, 0, 0, 0))
    st_m = pl.BlockSpec((1, 2, 8, LANES), lambda b, s: (b, 0, 0, 0))
    tok_f = pl.BlockSpec((1, M_CHUNK, BW), lambda b, s: (b, s, 0))
    tok_b = pl.BlockSpec((1, M_CHUNK, BW), lambda b, s: (b, nc - 1 - s, 0))
    kt_f = pl.BlockSpec((1, BW, M_CHUNK), lambda b, s: (b, 0, s))
    kt_b = pl.BlockSpec((1, BW, M_CHUNK), lambda b, s: (b, 0, nc - 1 - s))
    g_f = pl.BlockSpec((1, 1, 8, M_CHUNK), lambda b, s: (b, 0, 0, s))
    g_b = pl.BlockSpec((1, 1, 8, M_CHUNK), lambda b, s: (b, 1, 0, nc - 1 - s))
    return pl.pallas_call(
        _mlstm_kernel,
        out_shape=[jax.ShapeDtypeStruct((B, T, BW), F32),
                   jax.ShapeDtypeStruct(c0.shape, F32),
                   jax.ShapeDtypeStruct(m0.shape, F32)],
        grid=(B, nc),
        in_specs=[tok_f, tok_b, kt_f, kt_b, tok_f, tok_b, g_f, g_b, st_c, st_m],
        out_specs=[pl.BlockSpec((1, T, BW), lambda b, s: (b, 0, 0)), st_c, st_m],
        scratch_shapes=[pltpu.VMEM((2, M_HEADS, M_HDIM, 2 * M_HDIM), F32),
                        pltpu.VMEM((2, 8, LANES), F32)],
        compiler_params=_cparams(2),
        name="mlstm",
    )(qm, qm, kmt, kmt, vm, vm, g, g, c0, m0)


def _attn_kernel(sink_ref, q_ref, kc_ref, vc_ref, *rest, band):
    if band:
        kp_ref, kb_ref, kn_ref, vp_ref, vb_ref, vn_ref, o_ref = rest
    else:
        (o_ref,) = rest
    i = pl.program_id(1)
    nb = pl.num_programs(1)
    Q = A_BLOCK
    C = kc_ref.shape[1]
    lane = lax.broadcasted_iota(jnp.int32, (Q, LANES), 1)
    low = lane < A_HDIM
    if band:
        rr = lax.broadcasted_iota(jnp.int32, (A_GROUP * Q, Q), 0) & (Q - 1)
        cc = lax.broadcasted_iota(jnp.int32, (A_GROUP * Q, Q), 1)
        bias_prev = jnp.where((cc >= rr) & (i > 0), 0.0, NEG)
        bias_next = jnp.where((cc <= rr) & (i < nb - 1), 0.0, NEG)
    out_cols = [None] * (A_Q_HEADS // 2)

    def head_group(hk):
        ks = slice(hk * LANES, (hk + 1) * LANES)
        ones = jnp.ones((Q, LANES), BF16)
        if band:
            kcat = jnp.concatenate([kc_ref[0, :, ks], kb_ref[0, :, ks], kp_ref[0, :, ks], kn_ref[0, :, ks]], axis=0)
            vcat = jnp.concatenate([vc_ref[0, :, ks], vb_ref[0, :, ks], vp_ref[0, :, ks], vn_ref[0, :, ks]], axis=0)
        else:
            kcat = kc_ref[0, :, ks]
            vcat = vc_ref[0, :, ks]
        nk = kcat.shape[0]
        vaug = jnp.concatenate([vcat, jnp.concatenate([ones] * (nk // Q), axis=0)], axis=1)
        qs = []
        for gq in range(A_GROUP):
            hq = hk * A_GROUP + gq
            qcol = q_ref[0, :, (hq // 2) * LANES:(hq // 2 + 1) * LANES]
            keep = low if hq % 2 == 0 else jnp.logical_not(low)
            qs.append(jnp.where(keep, qcol, jnp.zeros_like(qcol)))
        qstack = jnp.concatenate(qs, axis=0)
        sc = _dot_nt(qstack, kcat)
        sink = jnp.concatenate(
            [jnp.full((Q, LANES), sink_ref[hk, gq] * LOG2E, F32) for gq in range(A_GROUP)], axis=0)
        yield
        cols = [sc[:, j * LANES:(j + 1) * LANES] for j in range(nk // LANES)]
        if band:
            cols[-2] = cols[-2] + bias_prev
            cols[-1] = cols[-1] + bias_next
        part = cols[0]
        for cj in cols[1:]:
            part = jnp.maximum(part, cj)
        m = jnp.maximum(jnp.max(part, axis=-1, keepdims=True), sink)
        p = jnp.concatenate([jnp.exp2(cj - m).astype(BF16) for cj in cols], axis=1)
        yield
        o = _dot(p, vaug)
        yield
        o = o[:, :LANES] / (o[:, LANES:] + jnp.exp2(sink - m))
        for pair in range(A_GROUP // 2):
            even = o[(2 * pair) * Q:(2 * pair + 1) * Q]
            odd = o[(2 * pair + 1) * Q:(2 * pair + 2) * Q]
            out_cols[hk * (A_GROUP // 2) + pair] = jnp.where(low, even, odd)
        yield

    groups = [head_group(hk) for hk in range(A_KV_HEADS)]
    for _ in range(4):
        for gr in groups:
            next(gr)
    o_ref[0] = jnp.concatenate(out_cols, axis=1).astype(o_ref.dtype)


def _attention(q, k, v, kc, vc, sink, band):
    B, T, _ = q.shape
    C = kc.shape[1]
    nb = T // A_BLOCK
    kvw = 2 * LANES
    in_specs = [pl.BlockSpec(memory_space=pltpu.SMEM),
                pl.BlockSpec((1, A_BLOCK, BW), lambda b, i: (b, i, 0)),
                pl.BlockSpec((1, C, kvw), lambda b, i: (b, 0, 0)),
                pl.BlockSpec((1, C, kvw), lambda b, i: (b, 0, 0))]
    args = [sink, q, kc, vc]
    if band:
        prev = pl.BlockSpec((1, A_BLOCK, kvw), lambda b, i: (b, jnp.maximum(i - 1, 0), 0))
        cur = pl.BlockSpec((1, A_BLOCK, kvw), lambda b, i: (b, i, 0))
        nxt = pl.BlockSpec((1, A_BLOCK, kvw), lambda b, i: (b, jnp.minimum(i + 1, nb - 1), 0))
        in_specs += [prev, cur, nxt, prev, cur, nxt]
        args += [k, k, k, v, v, v]
    return pl.pallas_call(
        functools.partial(_attn_kernel, band=band),
        out_shape=jax.ShapeDtypeStruct((B, T, BW), BF16),
        grid=(B, nb),
        in_specs=in_specs,
        out_specs=pl.BlockSpec((1, A_BLOCK, BW), lambda b, i: (b, i, 0)),
        compiler_params=_cparams(2),
        name="attn_band" if band else "attn_ctx",
    )(*args)


def _merge_kernel(x_ref, sh_ref, sc_ref, gt_ref, a_ref, ap_ref, an_ref, ic_ref, hm_ref, om_ref, at_ref,
                  wg_ref, pw_ref, ps_ref, nw_ref, wb_ref, wo_ref, lg_ref, lb_ref,
                  o_ref, abuf, *, alpha):
    i = pl.program_id(1)
    nt = pl.num_programs(1)
    tm = x_ref.shape[1]
    D = x_ref.shape[2]
    x = x_ref[0]
    h = _modulated(x, sh_ref[0], sc_ref[0]).astype(BF16)
    zh = [_dot(h, wg_ref[:, bi * D:(bi + 1) * D]) for bi in range(N_BRANCH)]
    br_attn = _dot(at_ref[0], wb_ref[2])

    abuf[0:HALO, :] = jnp.where(i > 0, ap_ref[0].astype(F32), 0.0)
    abuf[HALO:HALO + tm, :] = a_ref[0].astype(F32)
    abuf[HALO + tm:, :] = jnp.where(i < nt - 1, an_ref[0].astype(F32), 0.0)
    pooled = []
    for gi, w in enumerate(POOL_WINDOWS):
        cs = slice(gi * POOL_GDIM, (gi + 1) * POOL_GDIM)
        acc = abuf[HALO - w // 2:HALO - w // 2 + tm, cs]
        for k in range(1, w):
            acc = acc + abuf[HALO - w // 2 + k:HALO - w // 2 + k + tm, cs]
        u = acc * ic_ref[gi] - abuf[HALO:HALO + tm, cs]
        pooled.append(_dot(u.astype(BF16), pw_ref[gi]))
    pool = (jnp.concatenate(pooled, axis=1) * ps_ref[...]).astype(BF16)
    br_pool = _dot(pool, wb_ref[0])

    heads = [_layer_norm(hm_ref[0, :, hd * M_HDIM:(hd + 1) * M_HDIM]) for hd in range(M_HEADS)]
    ml = (jnp.concatenate(heads, axis=1) * nw_ref[...] * _sigmoid(om_ref[0].astype(F32))).astype(BF16)
    br_ml = _dot(ml, wb_ref[1])

    y2 = None
    for zi, bi in zip(zh, (br_pool, br_ml, br_attn)):
        yi = (jnp.tanh(zi) + 1.0) * bi
        y2 = yi if y2 is None else y2 + yi
    z2 = _dot(y2.astype(BF16), wo_ref[...])
    o_ref[0] = _layer_norm(alpha * x + (0.5 * gt_ref[0]) * z2) * lg_ref[...] + lb_ref[...]


def _pool_inv_counts(T):
    t = jnp.arange(T)
    rows = []
    for w in POOL_WINDOWS:
        cnt = jnp.clip(t - w // 2 + w, 0, T) - jnp.clip(t - w // 2, 0, T)
        rows.append(1.0 / cnt.astype(F32))
    return jnp.broadcast_to(jnp.stack(rows)[:, :, None], (len(POOL_WINDOWS), T, LANES))


def _merge(x, shift, scale, gate, a, hm, om, at, wg, pw, ps, nw, wb, wo, lg, lb, tm, alpha):
    B, T, D = x.shape
    per_b = shift.shape[0] > 1
    mod_map = (lambda b, i: (b, 0, 0)) if per_b else (lambda b, i: (0, 0, 0))
    tok = lambda n: pl.BlockSpec((1, tm, n), lambda b, i: (b, i, 0))
    full = lambda arr: pl.BlockSpec(arr.shape, lambda b, i: (0,) * arr.ndim)
    r = tm // HALO
    nh = T // HALO
    prev = pl.BlockSpec((1, HALO, BW), lambda b, i: (b, jnp.maximum(i * r - 1, 0), 0))
    nxt = pl.BlockSpec((1, HALO, BW), lambda b, i: (b, jnp.minimum((i + 1) * r, nh - 1), 0))
    mod = pl.BlockSpec((1, 1, D), mod_map)
    return pl.pallas_call(
        functools.partial(_merge_kernel, alpha=alpha),
        out_shape=jax.ShapeDtypeStruct((B, T, D), F32),
        grid=(B, T // tm),
        in_specs=[tok(D), mod, mod, mod, tok(BW), prev, nxt,
                  pl.BlockSpec((len(POOL_WINDOWS), tm, LANES), lambda b, i: (0, i, 0)),
                  tok(BW), tok(BW), tok(BW),
                  full(wg), full(pw), full(ps), full(nw), full(wb), full(wo), full(lg), full(lb)],
        out_specs=tok(D),
        scratch_shapes=[pltpu.VMEM((tm + 2 * HALO, BW), F32)],
        compiler_params=_cparams(2),
        name="merge",
    )(x, shift, scale, gate, a, a, a, _pool_inv_counts(T), hm, om, at, wg, pw, ps, nw, wb, wo, lg, lb)


def _ffn_kernel(x_ref, sh_ref, sc_ref, gt_ref, wg_ref, wu_ref, wd_ref, lg_ref, lb_ref, o_ref, *, alpha, fc):
    x = x_ref[0]
    h = _modulated(x, sh_ref[0], sc_ref[0]).astype(BF16)
    dff = wg_ref.shape[1]
    n = dff // fc

    def gate_up(j):
        cs = slice(j * fc, (j + 1) * fc)
        return _dot(h, wg_ref[:, cs]), _dot(h, wu_ref[:, cs])

    acc = None
    nxt = gate_up(0)
    for j in range(n):
        hg, up = nxt
        if j + 1 < n:
            nxt = gate_up(j + 1)
        act = (hg * (jnp.tanh(hg) + 1.0) * up).astype(BF16)
        part = _dot(act, wd_ref[j * fc:(j + 1) * fc, :])
        acc = part if acc is None else acc + part
    o_ref[0] = _layer_norm(alpha * x + gt_ref[0] * acc) * lg_ref[...] + lb_ref[...]


def _ffn(x, shift, scale, gate, wg, wu, wd, lg, lb, tm, alpha):
    B, T, D = x.shape
    per_b = shift.shape[0] > 1
    mod_map = (lambda b, i: (b, 0, 0)) if per_b else (lambda b, i: (0, 0, 0))
    tok = pl.BlockSpec((1, tm, D), lambda b, i: (b, i, 0))
    full = lambda arr: pl.BlockSpec(arr.shape, lambda b, i: (0,) * arr.ndim, pipeline_mode=pl.Buffered(1))
    mod = pl.BlockSpec((1, 1, D), mod_map)
    dff = wg.shape[1]
    fc = 2 * LANES if dff % (2 * LANES) == 0 else dff
    return pl.pallas_call(
        functools.partial(_ffn_kernel, alpha=alpha, fc=fc),
        out_shape=jax.ShapeDtypeStruct((B, T, D), F32),
        grid=(B, T // tm),
        in_specs=[tok, mod, mod, mod, full(wg), full(wu), full(wd), full(lg), full(lb)],
        out_specs=tok,
        compiler_params=_cparams(2),
        name="ffn",
    )(x, shift, scale, gate, wg, wu, wd, lg, lb)


def _rope_tables(T):
    t = jnp.arange(T)
    row = (t // GRID_W).astype(F32)
    col = (t % GRID_W).astype(F32)
    inv = ROPE_BASE ** (-jnp.arange(ROPE_PAIRS, dtype=F32) / ROPE_PAIRS)
    ang_r = row[:, None] * inv
    ang_c = col[:, None] * inv
    ang = jnp.concatenate([ang_r, ang_r, ang_c, ang_c], axis=-1)
    sign = jnp.concatenate([-jnp.ones(ROPE_PAIRS), jnp.ones(ROPE_PAIRS)] * 2).astype(F32)
    cos = jnp.tile(jnp.cos(ang), (1, 2))
    sin = jnp.tile(jnp.sin(ang) * sign, (1, 2))
    return cos, sin


def _pack_w_in(w_in_l):
    offs = np.cumsum([BW, BW, BW, BW, BW, 2 * M_HEADS, 2 * M_HEADS, BW, 2 * A_HDIM, 2 * A_HDIM])
    a, qm, km, vm, om, ig, fg, qa, ka, va, gates = jnp.split(w_in_l, offs.tolist(), axis=1)
    dup = lambda w: jnp.concatenate([w[:, :A_HDIM], w[:, :A_HDIM], w[:, A_HDIM:], w[:, A_HDIM:]], axis=1)
    w_main = jnp.concatenate([a, qm, vm, om, qa, dup(ka), dup(va)], axis=1).astype(BF16)
    h = M_HEADS
    w_gate_t = jnp.concatenate([ig[:, :h], fg[:, :h], ig[:, h:], fg[:, h:]], axis=1).T.astype(BF16)
    return w_main, km.T.astype(BF16), w_gate_t, (0.5 * gates).astype(BF16)


def kernel(x, c, ctx, c_ctx, w_mod, b_mod, w_in, pool_w, pool_scale, m_b_i, m_b_f, m_norm_w, attn_sink,
           w_branch, w_out, ln1_g, ln1_b, ln2_g, ln2_b, w_ffn_gate, w_ffn_up, w_ffn_down):
    B, T, D = x.shape
    C = ctx.shape[1]
    depth = w_mod.shape[0]
    alpha = (2 * depth) ** 0.25
    tm_x = 512 if T % 512 == 0 else T
    tm_f = 1024 if T % 1024 == 0 else tm_x
    tm_c = C
    h = M_HEADS

    rows = ((B + 1 + 7) // 8) * 8
    cc = jnp.zeros((rows, D), F32).at[:B].set(c).at[B].set(c_ctx)
    mods = _modulation(cc, w_mod, b_mod)
    cos_x, sin_x = _rope_tables(T)
    cos_c, sin_c = jnp.ones((C, LANES), F32), jnp.zeros((C, LANES), F32)
    zero_c = jnp.zeros((B, 2, h, M_HDIM, 2 * M_HDIM), F32)
    zero_m = jnp.zeros((B, 2, 8, LANES), F32)

    for l in range(depth):
        last = l == depth - 1
        mx = [mods[l, :B, j * D:(j + 1) * D].reshape(B, 1, D) for j in range(6)]
        mc = [mods[l, B:B + 1, j * D:(j + 1) * D].reshape(1, 1, D) for j in range(6)]
        w_main, w_kt, w_gate_t, w_bgate = _pack_w_in(w_in[l])
        bi, bf = m_b_i[l], m_b_f[l]
        bg = jnp.concatenate([bi[0], bf[0], bi[1], bf[1]]).reshape(16, 1)
        row = lambda v: v.reshape(1, -1)

        px = _project(x, mx[0], mx[1], cos_x, sin_x, w_main, w_kt, w_gate_t, bg, tm_f)
        pc = _project(ctx, mc[0], mc[1], cos_c, sin_c, w_main, w_kt, w_gate_t, bg, tm_c)
        a_x, qm_x, kt_x, vm_x, om_x, qa_x, ka_x, va_x, g_x = px
        a_c, qm_c, kt_c, vm_c, om_c, qa_c, ka_c, va_c, g_c = pc

        h_c, st_c, st_m = _mlstm(qm_c, kt_c, vm_c, g_c, zero_c, zero_m)
        h_x, _, _ = _mlstm(qm_x, kt_x, vm_x, g_x, st_c, st_m)
        at_x = _attention(qa_x, ka_x, va_x, ka_c, va_c, attn_sink[l], True)

        wargs = (w_bgate, pool_w[l].astype(BF16), row(pool_scale[l]), row(m_norm_w[l]),
                 w_branch[l].astype(BF16), w_out[l].astype(BF16), row(ln1_g[l]), row(ln1_b[l]))
        fargs = ((0.5 * w_ffn_gate[l]).astype(BF16), w_ffn_up[l].astype(BF16), w_ffn_down[l].astype(BF16),
                 row(ln2_g[l]), row(ln2_b[l]))
        x1 = _merge(x, mx[0], mx[1], mx[2], a_x, h_x, om_x, at_x, *wargs, tm_x, alpha)
        x_new = _ffn(x1, mx[3], mx[4], mx[5], *fargs, tm_f, alpha)
        if not last:
            at_c = _attention(qa_c, ka_c, va_c, ka_c, va_c, attn_sink[l], False)
            c1 = _merge(ctx, mc[0], mc[1], mc[2], a_c, h_c, om_c, at_c, *wargs, tm_c, alpha)
            ctx = _ffn(c1, mc[3], mc[4], mc[5], *fargs, tm_c, alpha)
        x = x_new
    return x
```

```python
import functools

import jax
import jax.numpy as jnp
import numpy as np
from jax import lax
from jax.experimental import pallas as pl
from jax.experimental.pallas import tpu as pltpu

F32 = jnp.float32
BF16 = jnp.bfloat16

GRID_W = 64
N_BRANCH = 3
POOL_WINDOWS = (2, 4, 8, 16)
POOL_GDIM = 128
M_HEADS = 4
M_HDIM = 128
M_CHUNK = 128
A_HDIM = 64
A_Q_HEADS = 8
A_KV_HEADS = 2
A_GROUP = 4
WINDOW = 128
A_BLOCK = 128
ROPE_BASE = 10000.0
ROPE_PAIRS = 16
BW = 512
LN_EPS = 1e-5
NEG = -1e30
LOG2E = 1.4426950408889634
LANES = 128
HALO = 16
PROJ_SLAB = 512
MIX_ROWS = 256
MERGE_SLAB = 256
VMEM_LIMIT_BYTES = 56 * 1024 * 1024


def _cparams(n_axes):
    return pltpu.CompilerParams(dimension_semantics=("arbitrary",) * n_axes,
                                vmem_limit_bytes=VMEM_LIMIT_BYTES)


def _dot(a, b):
    return jnp.dot(a, b, preferred_element_type=F32)


def _dot_nt(a, b):
    return lax.dot_general(a, b, (((1,), (1,)), ((), ())), preferred_element_type=F32)


def _dot_tn(a, b):
    return lax.dot_general(a, b, (((0,), (0,)), ((), ())), preferred_element_type=F32)


def _layer_norm(x):
    mu = jnp.mean(x, axis=-1, keepdims=True)
    xc = x - mu
    var = jnp.mean(xc * xc, axis=-1, keepdims=True)
    return xc * lax.rsqrt(var + LN_EPS)


def _modulated(x, shift, scale):
    return _layer_norm(x) * (1.0 + scale) + shift


def _sigmoid(x):
    return 0.5 * jnp.tanh(0.5 * x) + 0.5


def _mod_kernel(c_ref, w_ref, b_ref, o_ref):
    c = c_ref[...]
    s = (c * _sigmoid(c)).astype(BF16)
    o_ref[0] = _dot(s, w_ref[0].astype(BF16)) + b_ref[0]


def _modulation(cc, w_mod, b_mod):
    L, D, D6 = w_mod.shape
    R = cc.shape[0]
    tn = D
    return pl.pallas_call(
        _mod_kernel,
        out_shape=jax.ShapeDtypeStruct((L, R, D6), F32),
        grid=(L, D6 // tn),
        in_specs=[pl.BlockSpec((R, D), lambda l, j: (0, 0)),
                  pl.BlockSpec((1, D, tn), lambda l, j: (l, 0, j)),
                  pl.BlockSpec((1, 1, tn), lambda l, j: (l, 0, j))],
        out_specs=pl.BlockSpec((1, R, tn), lambda l, j: (l, 0, j)),
        compiler_params=_cparams(2),
        name="mod",
    )(cc, w_mod, b_mod.reshape(L, 1, D6))


def _swap16(x):
    lane = lax.broadcasted_iota(jnp.int32, x.shape, 1)
    up = pltpu.roll(x, LANES - ROPE_PAIRS, 1)
    dn = pltpu.roll(x, ROPE_PAIRS, 1)
    return jnp.where((lane & ROPE_PAIRS) == 0, up, dn)


def _rope(x, cos, sin):
    outs = []
    for j in range(x.shape[1] // LANES):
        xs = x[:, j * LANES:(j + 1) * LANES]
        outs.append(xs * cos + _swap16(xs) * sin)
    return outs[0] if len(outs) == 1 else jnp.concatenate(outs, axis=1)


def _dup_heads(x):
    lane = lax.broadcasted_iota(jnp.int32, x.shape, 1)
    sw = pltpu.roll(x, A_HDIM, 1)
    low = lane < A_HDIM
    return jnp.concatenate([jnp.where(low, x, sw), jnp.where(low, sw, x)], axis=1)


def _chunk_scan(v, reverse):
    lane = lax.broadcasted_iota(jnp.int32, v.shape, 1)
    sh = 1
    while sh < LANES:
        if reverse:
            v = v + jnp.where(lane < LANES - sh, pltpu.roll(v, LANES - sh, 1), 0.0)
        else:
            v = v + jnp.where(lane >= sh, pltpu.roll(v, sh, 1), 0.0)
        sh *= 2
    return v


def _proj_kernel(x_ref, sh_ref, sc_ref, cos_ref, sin_ref, w_ref, wk_ref, wg_ref, bg_ref,
                 a_ref, qm_ref, kt_ref, vm_ref, om_ref, qa_ref, ka_ref, va_ref, g_ref):
    tm = x_ref.shape[1]
    rows = min(tm, PROJ_SLAB)

    def slab(r0):
        rs = slice(r0, r0 + rows)
        h = _modulated(x_ref[0, rs, :], sh_ref[0], sc_ref[0]).astype(BF16)
        yield

        def cols(j, n=BW):
            return _dot(h, w_ref[:, j:j + n])

        a_ref[0, rs, :] = cols(0).astype(a_ref.dtype)
        qm_ref[0, rs, :] = cols(BW).astype(qm_ref.dtype)
        vm_ref[0, rs, :] = cols(2 * BW).astype(vm_ref.dtype)
        om_ref[0, rs, :] = cols(3 * BW).astype(om_ref.dtype)
        kt_ref[0, :, rs] = _dot_nt(wk_ref[...], h).astype(kt_ref.dtype)
        qa = cols(4 * BW)
        kv = cols(5 * BW, 2 * LANES)
        z = _dot_nt(wg_ref[...], h) + bg_ref[...]
        yield
        cos = cos_ref[rs, :]
        sin = sin_ref[rs, :]
        qa = _rope(qa, cos, sin) * (A_HDIM ** -0.5 * LOG2E)
        qa_ref[0, rs, :] = qa.astype(qa_ref.dtype)
        ka_ref[0, rs, :] = _dup_heads(_rope(kv[:, :LANES], cos, sin)).astype(ka_ref.dtype)
        va_ref[0, rs, :] = _dup_heads(kv[:, LANES:]).astype(va_ref.dtype)
        row = lax.broadcasted_iota(jnp.int32, (16, LANES), 0)
        is_f = (row & 4) != 0
        is_bwd = row >= 8
        for c in range(rows // LANES):
            zc = z[:, c * LANES:(c + 1) * LANES]
            lf = jnp.minimum(zc, 0.0) - jnp.log1p(jnp.exp(-jnp.abs(zc)))
            cum = jnp.where(is_bwd, _chunk_scan(lf, True), _chunk_scan(lf, False))
            out = jnp.where(is_f, cum, zc)
            g_ref[0, 0, :, r0 + c * LANES:r0 + (c + 1) * LANES] = out[:8]
            g_ref[0, 1, :, r0 + c * LANES:r0 + (c + 1) * LANES] = out[8:]
        yield

    slabs = [slab(r0) for r0 in range(0, tm, rows)]
    for _ in range(3):
        for sb in slabs:
            next(sb)


def _project(x, shift, scale, cos, sin, w, wk, wg, bg, tm):
    B, T, D = x.shape
    nw = w.shape[1]
    per_b = shift.shape[0] > 1
    mod_map = (lambda b, i: (b, 0, 0)) if per_b else (lambda b, i: (0, 0, 0))
    tok = lambda n: pl.BlockSpec((1, tm, n), lambda b, i: (b, i, 0))
    tshape = lambda n: jax.ShapeDtypeStruct((B, T, n), BF16)
    outs = [tshape(BW), tshape(BW), jax.ShapeDtypeStruct((B, BW, T), BF16), tshape(BW), tshape(BW), tshape(BW),
            tshape(2 * LANES), tshape(2 * LANES), jax.ShapeDtypeStruct((B, 2, 8, T), F32)]
    return pl.pallas_call(
        _proj_kernel,
        out_shape=outs,
        grid=(B, T // tm),
        in_specs=[tok(D),
                  pl.BlockSpec((1, 1, D), mod_map),
                  pl.BlockSpec((1, 1, D), mod_map),
                  pl.BlockSpec((tm, LANES), lambda b, i: (i, 0)),
                  pl.BlockSpec((tm, LANES), lambda b, i: (i, 0)),
                  pl.BlockSpec((D, nw), lambda b, i: (0, 0)),
                  pl.BlockSpec((BW, D), lambda b, i: (0, 0)),
                  pl.BlockSpec((16, D), lambda b, i: (0, 0)),
                  pl.BlockSpec((16, 1), lambda b, i: (0, 0))],
        out_specs=[tok(BW), tok(BW), pl.BlockSpec((1, BW, tm), lambda b, i: (b, 0, i)), tok(BW), tok(BW), tok(BW),
                   tok(2 * LANES), tok(2 * LANES),
                   pl.BlockSpec((1, 2, 8, tm), lambda b, i: (b, 0, 0, i))],
        compiler_params=_cparams(2),
        name="proj",
    )(x, shift, scale, cos, sin, w, wk, wg, bg)


def _split3(x):
    hi = x.astype(BF16).astype(F32)
    r1 = x - hi
    mid = r1.astype(BF16).astype(F32)
    return hi, mid, r1 - mid


def _mlstm_chain(rev, hd, n, q_ref, kt_ref, v_ref, g_ref, c_s, m_s, h_ref, row0):
    L = M_CHUNK
    dirn = 1 if rev else 0
    sl = slice(hd * M_HDIM, (hd + 1) * M_HDIM)
    kscale = M_HDIM ** -0.5
    rid = lax.broadcasted_iota(jnp.int32, (16, L), 0)
    m_prev = m_s[dirn, hd:hd + 1, :]
    c_prev = c_s[dirn, hd]
    items = []
    for ci in (reversed(range(n)) if rev else range(n)):
        rs = slice(ci * L, (ci + 1) * L)
        qh = q_ref[0, rs, sl]
        kt = kt_ref[0, sl, rs]
        vaug = jnp.concatenate([v_ref[0, rs, sl], jnp.ones((L, LANES), BF16)], axis=1)
        ig_row = g_ref[0, 0, hd:hd + 1, rs]
        b_row = g_ref[0, 0, 4 + hd:5 + hd, rs]
        b3 = _split3(b_row)
        r3 = _split3(ig_row - b_row)
        lhs = jnp.where(rid == 0, b3[0], jnp.where(rid == 1, b3[1], jnp.where(rid == 2, b3[2],
                        jnp.where(rid < 6, 1.0, 0.0))))
        rhs_l = jnp.where(rid < 3, 1.0, jnp.where(rid == 3, r3[0], jnp.where(rid == 4, r3[1],
                          jnp.where(rid == 5, r3[2], 0.0))))
        rhs_r = jnp.where(rid < 3, 1.0, 0.0)
        db = _dot_tn(lhs.astype(BF16), jnp.concatenate([rhs_l, rhs_r], axis=1).astype(BF16))
        qk = _dot(qh, kt)
        cross = _dot(qh, c_prev.astype(BF16))
        b_last = b_row[:, 0:1] if rev else b_row[:, L - 1:L]
        g_row = b_last - b_row + ig_row
        m_new = jnp.maximum(b_last + m_prev, jnp.max(g_row, axis=-1, keepdims=True))
        wk = jnp.exp(g_row - m_new) * kscale
        decay = jnp.exp(b_last + m_prev - m_new)
        upd = _dot((kt.astype(F32) * wk).astype(BF16), vaug)
        items.append(dict(ci=ci, vaug=vaug, db=db, qk=qk, cross=cross, m_prev=m_prev))
        c_prev = jnp.concatenate([decay, decay], axis=1) * c_prev + upd
        m_prev = m_new
    c_s[dirn, hd] = c_prev
    m_s[dirn, hd:hd + 1, :] = m_prev
    yield
    rr = lax.broadcasted_iota(jnp.int32, (L, L), 0)
    cc = lax.broadcasted_iota(jnp.int32, (L, L), 1)
    causal = (cc >= rr) if rev else (cc <= rr)
    for it in items:
        log_d = jnp.where(causal, it["db"][:, :L], NEG)
        inter = it["db"][:, L:] + it["m_prev"]
        m_row = jnp.maximum(inter, jnp.max(log_d, axis=-1, keepdims=True))
        it["p"] = (it["qk"] * (jnp.exp(log_d - m_row) * kscale)).astype(BF16)
        it["w_inter"] = jnp.exp(inter - m_row)
        it["m_row"] = m_row
    yield
    for it in items:
        it["intra"] = _dot(it["p"], it["vaug"])
    yield
    for it in items:
        intra, cross, w_inter = it["intra"], it["cross"], it["w_inter"]
        num = intra[:, :M_HDIM] + w_inter * cross[:, :M_HDIM]
        nq = intra[:, M_HDIM:] + w_inter * cross[:, M_HDIM:]
        den = jnp.maximum(jnp.abs(nq), jnp.exp(-it["m_row"]))
        h_ref[0, pl.ds(row0 + it["ci"] * L, L), sl] += num / den
    yield


def _attn_groups(sink_ref, q_ref, kc_ref, vc_ref, band_refs, o_ref, i, nsteps):
    band = band_refs is not None
    if band:
        kp_ref, kb_ref, kn_ref, vp_ref, vb_ref, vn_ref = band_refs
    Q = A_BLOCK
    ns = q_ref.shape[1] // Q
    lane = lax.broadcasted_iota(jnp.int32, (Q, LANES), 1)
    low = lane < A_HDIM
    if band:
        rr = lax.broadcasted_iota(jnp.int32, (A_GROUP * Q, Q), 0) & (Q - 1)
        cc = lax.broadcasted_iota(jnp.int32, (A_GROUP * Q, Q), 1)
        inner_prev = jnp.where(cc >= rr, 0.0, NEG)
        inner_next = jnp.where(cc <= rr, 0.0, NEG)
        edge_prev = jnp.where((cc >= rr) & (i > 0), 0.0, NEG)
        edge_next = jnp.where((cc <= rr) & (i < nsteps - 1), 0.0, NEG)
    out_cols = [[None] * (A_Q_HEADS // 2) for _ in range(ns)]

    def head_group(j, hk):
        ks = slice(hk * LANES, (hk + 1) * LANES)
        qr = slice(j * Q, (j + 1) * Q)
        ones = jnp.ones((Q, LANES), BF16)
        if band:
            blk = lambda ref, jj: ref[0, jj * Q:(jj + 1) * Q, ks]
            kprev, vprev = ((kp_ref[0, :, ks], vp_ref[0, :, ks]) if j == 0
                            else (blk(kb_ref, j - 1), blk(vb_ref, j - 1)))
            knext, vnext = ((kn_ref[0, :, ks], vn_ref[0, :, ks]) if j == ns - 1
                            else (blk(kb_ref, j + 1), blk(vb_ref, j + 1)))
            kcat = jnp.concatenate([kc_ref[0, :, ks], blk(kb_ref, j), kprev, knext], axis=0)
            vcat = jnp.concatenate([vc_ref[0, :, ks], blk(vb_ref, j), vprev, vnext], axis=0)
            bias_prev = edge_prev if j == 0 else inner_prev
            bias_next = edge_next if j == ns - 1 else inner_next
        else:
            kcat = kc_ref[0, :, ks]
            vcat = vc_ref[0, :, ks]
        nk = kcat.shape[0]
        vaug = jnp.concatenate([vcat, jnp.concatenate([ones] * (nk // Q), axis=0)], axis=1)
        qs = []
        for gq in range(A_GROUP):
            hq = hk * A_GROUP + gq
            qcol = q_ref[0, qr, (hq // 2) * LANES:(hq // 2 + 1) * LANES]
            keep = low if hq % 2 == 0 else jnp.logical_not(low)
            qs.append(jnp.where(keep, qcol, jnp.zeros_like(qcol)))
        qstack = jnp.concatenate(qs, axis=0)
        sc = _dot_nt(qstack, kcat)
        sink = jnp.concatenate(
            [jnp.full((Q, LANES), sink_ref[hk, gq] * LOG2E, F32) for gq in range(A_GROUP)], axis=0)
        yield
        cols = [sc[:, c * LANES:(c + 1) * LANES] for c in range(nk // LANES)]
        if band:
            cols[-2] = cols[-2] + bias_prev
            cols[-1] = cols[-1] + bias_next
        part = cols[0]
        for cj in cols[1:]:
            part = jnp.maximum(part, cj)
        m = jnp.maximum(jnp.max(part, axis=-1, keepdims=True), sink)
        p = jnp.concatenate([jnp.exp2(cj - m).astype(BF16) for cj in cols], axis=1)
        yield
        o = _dot(p, vaug)
        yield
        o = o[:, :LANES] / (o[:, LANES:] + jnp.exp2(sink - m))
        for pair in range(A_GROUP // 2):
            even = o[(2 * pair) * Q:(2 * pair + 1) * Q]
            odd = o[(2 * pair + 1) * Q:(2 * pair + 2) * Q]
            out_cols[j][hk * (A_GROUP // 2) + pair] = jnp.where(low, even, odd)
        yield

    def finalize():
        for j in range(ns):
            o_ref[0, j * Q:(j + 1) * Q, :] = jnp.concatenate(out_cols[j], axis=1).astype(o_ref.dtype)

    return [head_group(j, hk) for j in range(ns) for hk in range(A_KV_HEADS)], finalize


def _attn_specs(q, k, v, kc, vc, sink, band):
    T = q.shape[1]
    C = kc.shape[1]
    nb = T // A_BLOCK
    ns = MIX_ROWS // A_BLOCK
    kvw = 2 * LANES
    in_specs = [pl.BlockSpec(memory_space=pltpu.SMEM),
                pl.BlockSpec((1, MIX_ROWS, BW), lambda b, i: (b, i, 0)),
                pl.BlockSpec((1, C, kvw), lambda b, i: (b, 0, 0)),
                pl.BlockSpec((1, C, kvw), lambda b, i: (b, 0, 0))]
    args = [sink, q, kc, vc]
    if band:
        prev = pl.BlockSpec((1, A_BLOCK, kvw), lambda b, i: (b, jnp.maximum(ns * i - 1, 0), 0))
        cur = pl.BlockSpec((1, MIX_ROWS, kvw), lambda b, i: (b, i, 0))
        nxt = pl.BlockSpec((1, A_BLOCK, kvw), lambda b, i: (b, jnp.minimum(ns * (i + 1), nb - 1), 0))
        in_specs += [prev, cur, nxt, prev, cur, nxt]
        args += [k, k, k, v, v, v]
    return in_specs, args


def _mixer_kernel(*refs, attn, band):
    qf_ref, qb_ref, ktf_ref, ktb_ref, vf_ref, vb_ref, gf_ref, gb_ref, c0_ref, m0_ref = refs[:10]
    k = 10
    if attn:
        sink_ref, q_ref, kc_ref, vc_ref = refs[k:k + 4]
        k += 4
        band_refs = refs[k:k + 6] if band else None
        k += 6 if band else 0
    h_ref, cn_ref, mn_ref = refs[k:k + 3]
    k += 3
    if attn:
        o_ref = refs[k]
        k += 1
    c_s, m_s = refs[k:k + 2]

    s = pl.program_id(1)
    nc = pl.num_programs(1)
    rows = qf_ref.shape[1]
    n = rows // M_CHUNK

    @pl.when(s == 0)
    def _():
        c_s[...] = c0_ref[0]
        m_s[...] = m0_ref[0]
        h_ref[...] = jnp.zeros_like(h_ref)

    row_f = pl.multiple_of(s * rows, rows)
    row_b = pl.multiple_of((nc - 1 - s) * rows, rows)
    gens = []
    for hd in range(M_HEADS):
        gens.append(_mlstm_chain(False, hd, n, qf_ref, ktf_ref, vf_ref, gf_ref, c_s, m_s, h_ref, row_f))
        gens.append(_mlstm_chain(True, hd, n, qb_ref, ktb_ref, vb_ref, gb_ref, c_s, m_s, h_ref, row_b))
    if attn:
        groups, finalize = _attn_groups(sink_ref, q_ref, kc_ref, vc_ref, band_refs, o_ref, s, nc)
        step = len(gens) // len(groups)
        for gi, gr in enumerate(groups):
            gens.insert(gi * (step + 1) + step, gr)
    for _ in range(4):
        for ge in gens:
            next(ge)
    if attn:
        finalize()

    @pl.when(s == nc - 1)
    def _():
        cn_ref[0] = c_s[...]
        mn_ref[0] = m_s[...]


def _mixers(qm, kmt, vm, g, c0, m0, attn_args=None):
    B, T, _ = qm.shape
    rows = MIX_ROWS
    nc = T // rows
    st_c = pl.BlockSpec((1, 2, M_HEADS, M_HDIM, 2 * M_HDIM), lambda b, s: (b, 0, 0, 0, 0))
    st_m = pl.BlockSpec((1, 2, 8, LANES), lambda b, s: (b, 0, 0, 0))
    tok_f = pl.BlockSpec((1, rows, BW), lambda b, s: (b, s, 0))
    tok_b = pl.BlockSpec((1, rows, BW), lambda b, s: (b, nc - 1 - s, 0))
    kt_f = pl.BlockSpec((1, BW, rows), lambda b, s: (b, 0, s))
    kt_b = pl.BlockSpec((1, BW, rows), lambda b, s: (b, 0, nc - 1 - s))
    g_f = pl.BlockSpec((1, 1, 8, rows), lambda b, s: (b, 0, 0, s))
    g_b = pl.BlockSpec((1, 1, 8, rows), lambda b, s: (b, 1, 0, nc - 1 - s))
    in_specs = [tok_f, tok_b, kt_f, kt_b, tok_f, tok_b, g_f, g_b, st_c, st_m]
    args = [qm, qm, kmt, kmt, vm, vm, g, g, c0, m0]
    out_shape = [jax.ShapeDtypeStruct((B, T, BW), F32),
                 jax.ShapeDtypeStruct(c0.shape, F32),
                 jax.ShapeDtypeStruct(m0.shape, F32)]
    out_specs = [pl.BlockSpec((1, T, BW), lambda b, s: (b, 0, 0)), st_c, st_m]
    attn = attn_args is not None
    band = False
    if attn:
        band = attn_args[-1]
        a_specs, a_args = _attn_specs(*attn_args)
        in_specs += a_specs
        args += a_args
        out_shape.append(jax.ShapeDtypeStruct((B, T, BW), BF16))
        out_specs.append(pl.BlockSpec((1, rows, BW), lambda b, s: (b, s, 0)))
    return pl.pallas_call(
        functools.partial(_mixer_kernel, attn=attn, band=band),
        out_shape=out_shape,
        grid=(B, nc),
        in_specs=in_specs,
        out_specs=out_specs,
        scratch_shapes=[pltpu.VMEM((2, M_HEADS, M_HDIM, 2 * M_HDIM), F32),
                        pltpu.VMEM((2, 8, LANES), F32)],
        compiler_params=_cparams(2),
        name="mixers" if attn else "mlstm",
    )(*args)


def _merge_kernel(x_ref, sh_ref, sc_ref, gt_ref, a_ref, ap_ref, an_ref, ic_ref, hm_ref, om_ref, at_ref,
                  wg_ref, pw_ref, ps_ref, nw_ref, wb_ref, wo_ref, lg_ref, lb_ref,
                  o_ref, abuf, *, alpha):
    i = pl.program_id(1)
    nt = pl.num_programs(1)
    tm = x_ref.shape[1]
    D = x_ref.shape[2]
    rows = min(tm, MERGE_SLAB)
    abuf[0:HALO, :] = jnp.where(i > 0, ap_ref[0].astype(F32), 0.0)
    abuf[HALO:HALO + tm, :] = a_ref[0].astype(F32)
    abuf[HALO + tm:, :] = jnp.where(i < nt - 1, an_ref[0].astype(F32), 0.0)

    def slab(r0):
        rs = slice(r0, r0 + rows)
        x = x_ref[0, rs, :]
        h = _modulated(x, sh_ref[0], sc_ref[0]).astype(BF16)
        yield
        zh = [_dot(h, wg_ref[:, bi * D:(bi + 1) * D]) for bi in range(N_BRANCH)]
        br_attn = _dot(at_ref[0, rs, :], wb_ref[2])
        yield
        pooled = []
        for gi, w in enumerate(POOL_WINDOWS):
            cs = slice(gi * POOL_GDIM, (gi + 1) * POOL_GDIM)
            lo = HALO + r0 - w // 2
            acc = abuf[lo:lo + rows, cs]
            for k in range(1, w):
                acc = acc + abuf[lo + k:lo + k + rows, cs]
            u = acc * ic_ref[gi, rs, :] - abuf[HALO + r0:HALO + r0 + rows, cs]
            pooled.append(_dot(u.astype(BF16), pw_ref[gi]))
        pool = (jnp.concatenate(pooled, axis=1) * ps_ref[...]).astype(BF16)
        br_pool = _dot(pool, wb_ref[0])
        heads = [_layer_norm(hm_ref[0, rs, hd * M_HDIM:(hd + 1) * M_HDIM]) for hd in range(M_HEADS)]
        ml = (jnp.concatenate(heads, axis=1) * nw_ref[...] * _sigmoid(om_ref[0, rs, :].astype(F32))).astype(BF16)
        br_ml = _dot(ml, wb_ref[1])
        yield
        y2 = None
        for zi, bi in zip(zh, (br_pool, br_ml, br_attn)):
            yi = (jnp.tanh(zi) + 1.0) * bi
            y2 = yi if y2 is None else y2 + yi
        z2 = _dot(y2.astype(BF16), wo_ref[...])
        yield
        o_ref[0, rs, :] = _layer_norm(alpha * x + (0.5 * gt_ref[0]) * z2) * lg_ref[...] + lb_ref[...]
        yield

    slabs = [slab(r0) for r0 in range(0, tm, rows)]
    for _ in range(5):
        for sb in slabs:
            next(sb)


def _pool_inv_counts(T):
    t = jnp.arange(T)
    rows = []
    for w in POOL_WINDOWS:
        cnt = jnp.clip(t - w // 2 + w, 0, T) - jnp.clip(t - w // 2, 0, T)
        rows.append(1.0 / cnt.astype(F32))
    return jnp.broadcast_to(jnp.stack(rows)[:, :, None], (len(POOL_WINDOWS), T, LANES))


def _merge(x, shift, scale, gate, a, hm, om, at, wg, pw, ps, nw, wb, wo, lg, lb, tm, alpha):
    B, T, D = x.shape
    per_b = shift.shape[0] > 1
    mod_map = (lambda b, i: (b, 0, 0)) if per_b else (lambda b, i: (0, 0, 0))
    tok = lambda n: pl.BlockSpec((1, tm, n), lambda b, i: (b, i, 0))
    full = lambda arr: pl.BlockSpec(arr.shape, lambda b, i: (0,) * arr.ndim)
    r = tm // HALO
    nh = T // HALO
    prev = pl.BlockSpec((1, HALO, BW), lambda b, i: (b, jnp.maximum(i * r - 1, 0), 0))
    nxt = pl.BlockSpec((1, HALO, BW), lambda b, i: (b, jnp.minimum((i + 1) * r, nh - 1), 0))
    mod = pl.BlockSpec((1, 1, D), mod_map)
    return pl.pallas_call(
        functools.partial(_merge_kernel, alpha=alpha),
        out_shape=jax.ShapeDtypeStruct((B, T, D), F32),
        grid=(B, T // tm),
        in_specs=[tok(D), mod, mod, mod, tok(BW), prev, nxt,
                  pl.BlockSpec((len(POOL_WINDOWS), tm, LANES), lambda b, i: (0, i, 0)),
                  tok(BW), tok(BW), tok(BW),
                  full(wg), full(pw), full(ps), full(nw), full(wb), full(wo), full(lg), full(lb)],
        out_specs=tok(D),
        scratch_shapes=[pltpu.VMEM((tm + 2 * HALO, BW), F32)],
        compiler_params=_cparams(2),
        name="merge",
    )(x, shift, scale, gate, a, a, a, _pool_inv_counts(T), hm, om, at, wg, pw, ps, nw, wb, wo, lg, lb)


def _ffn_kernel(x_ref, sh_ref, sc_ref, gt_ref, wg_ref, wu_ref, wd_ref, lg_ref, lb_ref, o_ref, *, alpha, fc):
    x = x_ref[0]
    h = _modulated(x, sh_ref[0], sc_ref[0]).astype(BF16)
    dff = wg_ref.shape[1]
    n = dff // fc

    def gate_up(j):
        cs = slice(j * fc, (j + 1) * fc)
        return _dot(h, wg_ref[:, cs]), _dot(h, wu_ref[:, cs])

    acc = None
    nxt = gate_up(0)
    for j in range(n):
        hg, up = nxt
        if j + 1 < n:
            nxt = gate_up(j + 1)
        act = (hg * (jnp.tanh(hg) + 1.0) * up).astype(BF16)
        part = _dot(act, wd_ref[j * fc:(j + 1) * fc, :])
        acc = part if acc is None else acc + part
    o_ref[0] = _layer_norm(alpha * x + gt_ref[0] * acc) * lg_ref[...] + lb_ref[...]


def _ffn(x, shift, scale, gate, wg, wu, wd, lg, lb, tm, alpha):
    B, T, D = x.shape
    per_b = shift.shape[0] > 1
    mod_map = (lambda b, i: (b, 0, 0)) if per_b else (lambda b, i: (0, 0, 0))
    tok = pl.BlockSpec((1, tm, D), lambda b, i: (b, i, 0))
    full = lambda arr: pl.BlockSpec(arr.shape, lambda b, i: (0,) * arr.ndim, pipeline_mode=pl.Buffered(1))
    mod = pl.BlockSpec((1, 1, D), mod_map)
    dff = wg.shape[1]
    fc = 2 * LANES if dff % (2 * LANES) == 0 else dff
    return pl.pallas_call(
        functools.partial(_ffn_kernel, alpha=alpha, fc=fc),
        out_shape=jax.ShapeDtypeStruct((B, T, D), F32),
        grid=(B, T // tm),
        in_specs=[tok, mod, mod, mod, full(wg), full(wu), full(wd), full(lg), full(lb)],
        out_specs=tok,
        compiler_params=_cparams(2),
        name="ffn",
    )(x, shift, scale, gate, wg, wu, wd, lg, lb)


def _rope_tables(T):
    t = jnp.arange(T)
    row = (t // GRID_W).astype(F32)
    col = (t % GRID_W).astype(F32)
    inv = ROPE_BASE ** (-jnp.arange(ROPE_PAIRS, dtype=F32) / ROPE_PAIRS)
    ang_r = row[:, None] * inv
    ang_c = col[:, None] * inv
    ang = jnp.concatenate([ang_r, ang_r, ang_c, ang_c], axis=-1)
    sign = jnp.concatenate([-jnp.ones(ROPE_PAIRS), jnp.ones(ROPE_PAIRS)] * 2).astype(F32)
    cos = jnp.tile(jnp.cos(ang), (1, 2))
    sin = jnp.tile(jnp.sin(ang) * sign, (1, 2))
    return cos, sin


def _pack_w_in(w_in_l):
    offs = np.cumsum([BW, BW, BW, BW, BW, 2 * M_HEADS, 2 * M_HEADS, BW, 2 * A_HDIM, 2 * A_HDIM])
    a, qm, km, vm, om, ig, fg, qa, ka, va, gates = jnp.split(w_in_l, offs.tolist(), axis=1)
    w_main = jnp.concatenate([a, qm, vm, om, qa, ka, va], axis=1).astype(BF16)
    h = M_HEADS
    w_gate_t = jnp.concatenate([ig[:, :h], fg[:, :h], ig[:, h:], fg[:, h:]], axis=1).T.astype(BF16)
    return w_main, km.T.astype(BF16), w_gate_t, (0.5 * gates).astype(BF16)


def kernel(x, c, ctx, c_ctx, w_mod, b_mod, w_in, pool_w, pool_scale, m_b_i, m_b_f, m_norm_w, attn_sink,
           w_branch, w_out, ln1_g, ln1_b, ln2_g, ln2_b, w_ffn_gate, w_ffn_up, w_ffn_down):
    B, T, D = x.shape
    C = ctx.shape[1]
    depth = w_mod.shape[0]
    alpha = (2 * depth) ** 0.25
    tm_x = 512 if T % 512 == 0 else T
    tm_f = 1024 if T % 1024 == 0 else tm_x
    tm_c = C
    h = M_HEADS

    rows = ((B + 1 + 7) // 8) * 8
    cc = jnp.zeros((rows, D), F32).at[:B].set(c).at[B].set(c_ctx)
    mods = _modulation(cc, w_mod, b_mod)
    cos_x, sin_x = _rope_tables(T)
    cos_c, sin_c = jnp.ones((C, LANES), F32), jnp.zeros((C, LANES), F32)
    zero_c = jnp.zeros((B, 2, h, M_HDIM, 2 * M_HDIM), F32)
    zero_m = jnp.zeros((B, 2, 8, LANES), F32)

    for l in range(depth):
        last = l == depth - 1
        mx = [mods[l, :B, j * D:(j + 1) * D].reshape(B, 1, D) for j in range(6)]
        mc = [mods[l, B:B + 1, j * D:(j + 1) * D].reshape(1, 1, D) for j in range(6)]
        w_main, w_kt, w_gate_t, w_bgate = _pack_w_in(w_in[l])
        bi, bf = m_b_i[l], m_b_f[l]
        bg = jnp.concatenate([bi[0], bf[0], bi[1], bf[1]]).reshape(16, 1)
        row = lambda v: v.reshape(1, -1)

        px = _project(x, mx[0], mx[1], cos_x, sin_x, w_main, w_kt, w_gate_t, bg, tm_f)
        pc = _project(ctx, mc[0], mc[1], cos_c, sin_c, w_main, w_kt, w_gate_t, bg, tm_c)
        a_x, qm_x, kt_x, vm_x, om_x, qa_x, ka_x, va_x, g_x = px
        a_c, qm_c, kt_c, vm_c, om_c, qa_c, ka_c, va_c, g_c = pc

        ctx_attn = None if last else (qa_c, ka_c, va_c, ka_c, va_c, attn_sink[l], False)
        mix_c = _mixers(qm_c, kt_c, vm_c, g_c, zero_c, zero_m, ctx_attn)
        h_c, st_c, st_m = mix_c[:3]
        h_x, _, _, at_x = _mixers(qm_x, kt_x, vm_x, g_x, st_c, st_m,
                                  (qa_x, ka_x, va_x, ka_c, va_c, attn_sink[l], True))

        wargs = (w_bgate, pool_w[l].astype(BF16), row(pool_scale[l]), row(m_norm_w[l]),
                 w_branch[l].astype(BF16), w_out[l].astype(BF16), row(ln1_g[l]), row(ln1_b[l]))
        fargs = ((0.5 * w_ffn_gate[l]).astype(BF16), w_ffn_up[l].astype(BF16), w_ffn_down[l].astype(BF16),
                 row(ln2_g[l]), row(ln2_b[l]))
        x1 = _merge(x, mx[0], mx[1], mx[2], a_x, h_x, om_x, at_x, *wargs, tm_x, alpha)
        x_new = _ffn(x1, mx[3], mx[4], mx[5], *fargs, tm_f, alpha)
        if not last:
            at_c = mix_c[3]
            c1 = _merge(ctx, mc[0], mc[1], mc[2], a_c, h_c, om_c, at_c, *wargs, tm_c, alpha)
            ctx = _ffn(c1, mc[3], mc[4], mc[5], *fargs, tm_c, alpha)
        x = x_new
    return x
```

```python
import functools

import jax
import jax.numpy as jnp
import numpy as np
from jax import lax
from jax.experimental import pallas as pl
from jax.experimental.pallas import tpu as pltpu

F32 = jnp.float32
BF16 = jnp.bfloat16

GRID_W = 64
N_BRANCH = 3
POOL_WINDOWS = (2, 4, 8, 16)
POOL_GDIM = 128
M_HEADS = 4
M_HDIM = 128
M_CHUNK = 128
A_HDIM = 64
A_Q_HEADS = 8
A_KV_HEADS = 2
A_GROUP = 4
WINDOW = 128
A_BLOCK = 128
ROPE_BASE = 10000.0
ROPE_PAIRS = 16
BW = 512
LN_EPS = 1e-5
NEG = -1e30
LOG2E = 1.4426950408889634
LANES = 128
HALO = 16
PROJ_SLAB = 512
MIX_ROWS = 256
MERGE_SLAB = 256
FFN_SLAB = 512
VMEM_LIMIT_BYTES = 56 * 1024 * 1024


def _cparams(n_axes):
    return pltpu.CompilerParams(dimension_semantics=("arbitrary",) * n_axes,
                                vmem_limit_bytes=VMEM_LIMIT_BYTES)


def _dot(a, b):
    return jnp.dot(a, b, preferred_element_type=F32)


def _dot_nt(a, b):
    return lax.dot_general(a, b, (((1,), (1,)), ((), ())), preferred_element_type=F32)


def _dot_tn(a, b):
    return lax.dot_general(a, b, (((0,), (0,)), ((), ())), preferred_element_type=F32)


def _layer_norm(x):
    mu = jnp.mean(x, axis=-1, keepdims=True)
    xc = x - mu
    var = jnp.mean(xc * xc, axis=-1, keepdims=True)
    return xc * lax.rsqrt(var + LN_EPS)


def _modulated(x, shift, scale):
    return _layer_norm(x) * (1.0 + scale) + shift


def _sigmoid(x):
    return 0.5 * jnp.tanh(0.5 * x) + 0.5


def _mod_kernel(c_ref, w_ref, b_ref, o_ref):
    c = c_ref[...]
    s = (c * _sigmoid(c)).astype(BF16)
    o_ref[0] = _dot(s, w_ref[0].astype(BF16)) + b_ref[0]


def _modulation(cc, w_mod, b_mod):
    L, D, D6 = w_mod.shape
    R = cc.shape[0]
    tn = D
    return pl.pallas_call(
        _mod_kernel,
        out_shape=jax.ShapeDtypeStruct((L, R, D6), F32),
        grid=(L, D6 // tn),
        in_specs=[pl.BlockSpec((R, D), lambda l, j: (0, 0)),
                  pl.BlockSpec((1, D, tn), lambda l, j: (l, 0, j)),
                  pl.BlockSpec((1, 1, tn), lambda l, j: (l, 0, j))],
        out_specs=pl.BlockSpec((1, R, tn), lambda l, j: (l, 0, j)),
        compiler_params=_cparams(2),
        name="mod",
    )(cc, w_mod, b_mod.reshape(L, 1, D6))


def _swap16(x):
    lane = lax.broadcasted_iota(jnp.int32, x.shape, 1)
    up = pltpu.roll(x, LANES - ROPE_PAIRS, 1)
    dn = pltpu.roll(x, ROPE_PAIRS, 1)
    return jnp.where((lane & ROPE_PAIRS) == 0, up, dn)


def _rope(x, cos, sin):
    outs = []
    for j in range(x.shape[1] // LANES):
        xs = x[:, j * LANES:(j + 1) * LANES]
        outs.append(xs * cos + _swap16(xs) * sin)
    return outs[0] if len(outs) == 1 else jnp.concatenate(outs, axis=1)


def _dup_heads(x):
    lane = lax.broadcasted_iota(jnp.int32, x.shape, 1)
    sw = pltpu.roll(x, A_HDIM, 1)
    low = lane < A_HDIM
    return jnp.concatenate([jnp.where(low, x, sw), jnp.where(low, sw, x)], axis=1)


def _chunk_scan(v, reverse):
    lane = lax.broadcasted_iota(jnp.int32, v.shape, 1)
    sh = 1
    while sh < LANES:
        if reverse:
            v = v + jnp.where(lane < LANES - sh, pltpu.roll(v, LANES - sh, 1), 0.0)
        else:
            v = v + jnp.where(lane >= sh, pltpu.roll(v, sh, 1), 0.0)
        sh *= 2
    return v


def _proj_kernel(x_ref, sh_ref, sc_ref, cos_ref, sin_ref, w_ref, wk_ref, wg_ref, bg_ref,
                 a_ref, qm_ref, kt_ref, vm_ref, om_ref, qa_ref, ka_ref, va_ref, g_ref):
    tm = x_ref.shape[1]
    rows = min(tm, PROJ_SLAB)

    def slab(r0):
        rs = slice(r0, r0 + rows)
        h = _modulated(x_ref[0, rs, :], sh_ref[0], sc_ref[0]).astype(BF16)
        yield

        def cols(j, n=BW):
            return _dot(h, w_ref[:, j:j + n])

        a_ref[0, rs, :] = cols(0).astype(a_ref.dtype)
        qm_ref[0, rs, :] = cols(BW).astype(qm_ref.dtype)
        vm_ref[0, rs, :] = cols(2 * BW).astype(vm_ref.dtype)
        om_ref[0, rs, :] = cols(3 * BW).astype(om_ref.dtype)
        kt_ref[0, :, rs] = _dot_nt(wk_ref[...], h).astype(kt_ref.dtype)
        qa = cols(4 * BW)
        kv = cols(5 * BW, 2 * LANES)
        z = _dot_nt(wg_ref[...], h) + bg_ref[...]
        yield
        cos = cos_ref[rs, :]
        sin = sin_ref[rs, :]
        qa = _rope(qa, cos, sin) * (A_HDIM ** -0.5 * LOG2E)
        qa_ref[0, rs, :] = qa.astype(qa_ref.dtype)
        ka_ref[0, rs, :] = _dup_heads(_rope(kv[:, :LANES], cos, sin)).astype(ka_ref.dtype)
        va_ref[0, rs, :] = _dup_heads(kv[:, LANES:]).astype(va_ref.dtype)
        row = lax.broadcasted_iota(jnp.int32, (16, LANES), 0)
        is_f = (row & 4) != 0
        is_bwd = row >= 8
        for c in range(rows // LANES):
            zc = z[:, c * LANES:(c + 1) * LANES]
            lf = jnp.minimum(zc, 0.0) - jnp.log1p(jnp.exp(-jnp.abs(zc)))
            cum = jnp.where(is_bwd, _chunk_scan(lf, True), _chunk_scan(lf, False))
            out = jnp.where(is_f, cum, zc)
            g_ref[0, 0, :, r0 + c * LANES:r0 + (c + 1) * LANES] = out[:8]
            g_ref[0, 1, :, r0 + c * LANES:r0 + (c + 1) * LANES] = out[8:]
        yield

    slabs = [slab(r0) for r0 in range(0, tm, rows)]
    for _ in range(3):
        for sb in slabs:
            next(sb)


def _project(x, shift, scale, cos, sin, w, wk, wg, bg, tm):
    B, T, D = x.shape
    nw = w.shape[1]
    per_b = shift.shape[0] > 1
    mod_map = (lambda b, i: (b, 0, 0)) if per_b else (lambda b, i: (0, 0, 0))
    tok = lambda n: pl.BlockSpec((1, tm, n), lambda b, i: (b, i, 0))
    tshape = lambda n: jax.ShapeDtypeStruct((B, T, n), BF16)
    outs = [tshape(BW), tshape(BW), jax.ShapeDtypeStruct((B, BW, T), BF16), tshape(BW), tshape(BW), tshape(BW),
            tshape(2 * LANES), tshape(2 * LANES), jax.ShapeDtypeStruct((B, 2, 8, T), F32)]
    return pl.pallas_call(
        _proj_kernel,
        out_shape=outs,
        grid=(B, T // tm),
        in_specs=[tok(D),
                  pl.BlockSpec((1, 1, D), mod_map),
                  pl.BlockSpec((1, 1, D), mod_map),
                  pl.BlockSpec((tm, LANES), lambda b, i: (i, 0)),
                  pl.BlockSpec((tm, LANES), lambda b, i: (i, 0)),
                  pl.BlockSpec((D, nw), lambda b, i: (0, 0)),
                  pl.BlockSpec((BW, D), lambda b, i: (0, 0)),
                  pl.BlockSpec((16, D), lambda b, i: (0, 0)),
                  pl.BlockSpec((16, 1), lambda b, i: (0, 0))],
        out_specs=[tok(BW), tok(BW), pl.BlockSpec((1, BW, tm), lambda b, i: (b, 0, i)), tok(BW), tok(BW), tok(BW),
                   tok(2 * LANES), tok(2 * LANES),
                   pl.BlockSpec((1, 2, 8, tm), lambda b, i: (b, 0, 0, i))],
        compiler_params=_cparams(2),
        name="proj",
    )(x, shift, scale, cos, sin, w, wk, wg, bg)


def _split3(x):
    hi = x.astype(BF16).astype(F32)
    r1 = x - hi
    mid = r1.astype(BF16).astype(F32)
    return hi, mid, r1 - mid


def _mlstm_chain(rev, hd, n, q_ref, kt_ref, v_ref, g_ref, c_s, m_s, h_ref, row0):
    L = M_CHUNK
    dirn = 1 if rev else 0
    sl = slice(hd * M_HDIM, (hd + 1) * M_HDIM)
    kscale = M_HDIM ** -0.5
    rid = lax.broadcasted_iota(jnp.int32, (16, L), 0)
    m_prev = m_s[dirn, hd:hd + 1, :]
    c_prev = c_s[dirn, hd]
    items = []
    for ci in (reversed(range(n)) if rev else range(n)):
        rs = slice(ci * L, (ci + 1) * L)
        qh = q_ref[0, rs, sl]
        kt = kt_ref[0, sl, rs]
        vaug = jnp.concatenate([v_ref[0, rs, sl], jnp.ones((L, LANES), BF16)], axis=1)
        ig_row = g_ref[0, 0, hd:hd + 1, rs]
        b_row = g_ref[0, 0, 4 + hd:5 + hd, rs]
        b3 = _split3(b_row)
        r3 = _split3(ig_row - b_row)
        lhs = jnp.where(rid == 0, b3[0], jnp.where(rid == 1, b3[1], jnp.where(rid == 2, b3[2],
                        jnp.where(rid < 6, 1.0, 0.0))))
        rhs_l = jnp.where(rid < 3, 1.0, jnp.where(rid == 3, r3[0], jnp.where(rid == 4, r3[1],
                          jnp.where(rid == 5, r3[2], 0.0))))
        rhs_r = jnp.where(rid < 3, 1.0, 0.0)
        db = _dot_tn(lhs.astype(BF16), jnp.concatenate([rhs_l, rhs_r], axis=1).astype(BF16))
        qk = _dot(qh, kt)
        b_last = b_row[:, 0:1] if rev else b_row[:, L - 1:L]
        g_row = b_last - b_row + ig_row
        m_new = jnp.maximum(b_last + m_prev, jnp.max(g_row, axis=-1, keepdims=True))
        wk = jnp.exp(g_row - m_new) * kscale
        decay = jnp.exp(b_last + m_prev - m_new)
        upd = _dot((kt.astype(F32) * wk).astype(BF16), vaug)
        rhs = jnp.concatenate([vaug, c_prev.astype(BF16)], axis=0)
        items.append(dict(ci=ci, qh=qh, rhs=rhs, db=db, qk=qk, m_prev=m_prev))
        c_prev = jnp.concatenate([decay, decay], axis=1) * c_prev + upd
        m_prev = m_new
    c_s[dirn, hd] = c_prev
    m_s[dirn, hd:hd + 1, :] = m_prev
    yield
    rr = lax.broadcasted_iota(jnp.int32, (L, L), 0)
    cc = lax.broadcasted_iota(jnp.int32, (L, L), 1)
    causal = (cc >= rr) if rev else (cc <= rr)
    for it in items:
        log_d = jnp.where(causal, it["db"][:, :L], NEG)
        inter = it["db"][:, L:] + it["m_prev"]
        m_row = jnp.maximum(inter, jnp.max(log_d, axis=-1, keepdims=True))
        p = (it["qk"] * (jnp.exp(log_d - m_row) * kscale)).astype(BF16)
        qw = (it["qh"].astype(F32) * jnp.exp(inter - m_row)).astype(BF16)
        it["lhs"] = jnp.concatenate([p, qw], axis=1)
        it["m_row"] = m_row
    yield
    for it in items:
        it["tot"] = _dot(it["lhs"], it["rhs"])
    yield
    for it in items:
        tot = it["tot"]
        den = jnp.maximum(jnp.abs(tot[:, M_HDIM:]), jnp.exp(-it["m_row"]))
        h_ref[0, pl.ds(row0 + it["ci"] * L, L), sl] += tot[:, :M_HDIM] / den
    yield


def _attn_groups(sink_ref, q_ref, kc_ref, vc_ref, band_refs, o_ref, i, nsteps):
    band = band_refs is not None
    if band:
        kp_ref, kb_ref, kn_ref, vp_ref, vb_ref, vn_ref = band_refs
    Q = A_BLOCK
    ns = q_ref.shape[1] // Q
    NP = A_GROUP // 2
    if band:
        rr = lax.broadcasted_iota(jnp.int32, (NP * Q, Q), 0) & (Q - 1)
        cc = lax.broadcasted_iota(jnp.int32, (NP * Q, Q), 1)
        inner_prev = jnp.where(cc >= rr, 0.0, NEG)
        inner_next = jnp.where(cc <= rr, 0.0, NEG)
        edge_prev = jnp.where((cc >= rr) & (i > 0), 0.0, NEG)
        edge_next = jnp.where((cc <= rr) & (i < nsteps - 1), 0.0, NEG)
    out_cols = [[None] * (A_Q_HEADS // 2) for _ in range(ns)]

    def head_group(j, hk):
        ks = slice(hk * LANES, (hk + 1) * LANES)
        qr = slice(j * Q, (j + 1) * Q)
        if band:
            blk = lambda ref, jj: ref[0, jj * Q:(jj + 1) * Q, ks]
            kprev, vprev = ((kp_ref[0, :, ks], vp_ref[0, :, ks]) if j == 0
                            else (blk(kb_ref, j - 1), blk(vb_ref, j - 1)))
            knext, vnext = ((kn_ref[0, :, ks], vn_ref[0, :, ks]) if j == ns - 1
                            else (blk(kb_ref, j + 1), blk(vb_ref, j + 1)))
            kcat = jnp.concatenate([kc_ref[0, :, ks], blk(kb_ref, j), kprev, knext], axis=0)
            vcat = jnp.concatenate([vc_ref[0, :, ks], blk(vb_ref, j), vprev, vnext], axis=0)
            bias_prev = edge_prev if j == 0 else inner_prev
            bias_next = edge_next if j == ns - 1 else inner_next
        else:
            kcat = kc_ref[0, :, ks]
            vcat = vc_ref[0, :, ks]
        nk = kcat.shape[0]
        lane = lax.broadcasted_iota(jnp.int32, (nk, LANES), 1)
        m_lo = jnp.where(lane < A_HDIM, 1.0, 0.0).astype(BF16)
        m_hi = jnp.where(lane < A_HDIM, 0.0, 1.0).astype(BF16)
        k_bd = jnp.concatenate([kcat * m_lo, kcat * m_hi], axis=0)
        v_bd = jnp.concatenate([jnp.concatenate([vcat * m_lo, m_lo], axis=1),
                                jnp.concatenate([vcat * m_hi, m_hi], axis=1)], axis=0)
        qstack = jnp.concatenate(
            [q_ref[0, qr, (hk * NP + pr) * LANES:(hk * NP + pr + 1) * LANES] for pr in range(NP)], axis=0)
        sc = _dot_nt(qstack, k_bd)
        sinks = [jnp.concatenate([jnp.full((Q, LANES), sink_ref[hk, 2 * pr + par] * LOG2E, F32)
                                  for pr in range(NP)], axis=0) for par in range(2)]
        yield
        ps, ms = [], []
        for par in range(2):
            cols = [sc[:, par * nk + c * LANES:par * nk + (c + 1) * LANES] for c in range(nk // LANES)]
            if band:
                cols[-2] = cols[-2] + bias_prev
                cols[-1] = cols[-1] + bias_next
            part = cols[0]
            for cj in cols[1:]:
                part = jnp.maximum(part, cj)
            m = jnp.maximum(jnp.max(part, axis=-1, keepdims=True), sinks[par])
            ps += [jnp.exp2(cj - m).astype(BF16) for cj in cols]
            ms.append(m)
        p = jnp.concatenate(ps, axis=1)
        yield
        o = _dot(p, v_bd)
        yield
        lane_q = lax.broadcasted_iota(jnp.int32, (NP * Q, LANES), 1)
        sink_term = jnp.where(lane_q < A_HDIM, jnp.exp2(sinks[0] - ms[0]), jnp.exp2(sinks[1] - ms[1]))
        o = o[:, :LANES] / (o[:, LANES:] + sink_term)
        for pr in range(NP):
            out_cols[j][hk * NP + pr] = o[pr * Q:(pr + 1) * Q]
        yield

    def finalize():
        for j in range(ns):
            o_ref[0, j * Q:(j + 1) * Q, :] = jnp.concatenate(out_cols[j], axis=1).astype(o_ref.dtype)

    return [head_group(j, hk) for j in range(ns) for hk in range(A_KV_HEADS)], finalize


def _attn_specs(q, k, v, kc, vc, sink, band):
    T = q.shape[1]
    C = kc.shape[1]
    nb = T // A_BLOCK
    ns = MIX_ROWS // A_BLOCK
    kvw = 2 * LANES
    in_specs = [pl.BlockSpec(memory_space=pltpu.SMEM),
                pl.BlockSpec((1, MIX_ROWS, BW), lambda b, i: (b, i, 0)),
                pl.BlockSpec((1, C, kvw), lambda b, i: (b, 0, 0)),
                pl.BlockSpec((1, C, kvw), lambda b, i: (b, 0, 0))]
    args = [sink, q, kc, vc]
    if band:
        prev = pl.BlockSpec((1, A_BLOCK, kvw), lambda b, i: (b, jnp.maximum(ns * i - 1, 0), 0))
        cur = pl.BlockSpec((1, MIX_ROWS, kvw), lambda b, i: (b, i, 0))
        nxt = pl.BlockSpec((1, A_BLOCK, kvw), lambda b, i: (b, jnp.minimum(ns * (i + 1), nb - 1), 0))
        in_specs += [prev, cur, nxt, prev, cur, nxt]
        args += [k, k, k, v, v, v]
    return in_specs, args


def _mixer_kernel(*refs, attn, band):
    qf_ref, qb_ref, ktf_ref, ktb_ref, vf_ref, vb_ref, gf_ref, gb_ref, c0_ref, m0_ref = refs[:10]
    k = 10
    if attn:
        sink_ref, q_ref, kc_ref, vc_ref = refs[k:k + 4]
        k += 4
        band_refs = refs[k:k + 6] if band else None
        k += 6 if band else 0
    h_ref, cn_ref, mn_ref = refs[k:k + 3]
    k += 3
    if attn:
        o_ref = refs[k]
        k += 1
    c_s, m_s = refs[k:k + 2]

    s = pl.program_id(1)
    nc = pl.num_programs(1)
    rows = qf_ref.shape[1]
    n = rows // M_CHUNK

    @pl.when(s == 0)
    def _():
        c_s[...] = c0_ref[0]
        m_s[...] = m0_ref[0]
        h_ref[...] = jnp.zeros_like(h_ref)

    row_f = pl.multiple_of(s * rows, rows)
    row_b = pl.multiple_of((nc - 1 - s) * rows, rows)
    gens = []
    for hd in range(M_HEADS):
        gens.append(_mlstm_chain(False, hd, n, qf_ref, ktf_ref, vf_ref, gf_ref, c_s, m_s, h_ref, row_f))
        gens.append(_mlstm_chain(True, hd, n, qb_ref, ktb_ref, vb_ref, gb_ref, c_s, m_s, h_ref, row_b))
    if attn:
        groups, finalize = _attn_groups(sink_ref, q_ref, kc_ref, vc_ref, band_refs, o_ref, s, nc)
        step = len(gens) // len(groups)
        for gi, gr in enumerate(groups):
            gens.insert(gi * (step + 1) + step, gr)
    for _ in range(4):
        for ge in gens:
            next(ge)
    if attn:
        finalize()

    @pl.when(s == nc - 1)
    def _():
        cn_ref[0] = c_s[...]
        mn_ref[0] = m_s[...]


def _mixers(qm, kmt, vm, g, c0, m0, attn_args=None):
    B, T, _ = qm.shape
    rows = MIX_ROWS
    nc = T // rows
    st_c = pl.BlockSpec((1, 2, M_HEADS, M_HDIM, 2 * M_HDIM), lambda b, s: (b, 0, 0, 0, 0))
    st_m = pl.BlockSpec((1, 2, 8, LANES), lambda b, s: (b, 0, 0, 0))
    tok_f = pl.BlockSpec((1, rows, BW), lambda b, s: (b, s, 0))
    tok_b = pl.BlockSpec((1, rows, BW), lambda b, s: (b, nc - 1 - s, 0))
    kt_f = pl.BlockSpec((1, BW, rows), lambda b, s: (b, 0, s))
    kt_b = pl.BlockSpec((1, BW, rows), lambda b, s: (b, 0, nc - 1 - s))
    g_f = pl.BlockSpec((1, 1, 8, rows), lambda b, s: (b, 0, 0, s))
    g_b = pl.BlockSpec((1, 1, 8, rows), lambda b, s: (b, 1, 0, nc - 1 - s))
    in_specs = [tok_f, tok_b, kt_f, kt_b, tok_f, tok_b, g_f, g_b, st_c, st_m]
    args = [qm, qm, kmt, kmt, vm, vm, g, g, c0, m0]
    out_shape = [jax.ShapeDtypeStruct((B, T, BW), F32),
                 jax.ShapeDtypeStruct(c0.shape, F32),
                 jax.ShapeDtypeStruct(m0.shape, F32)]
    out_specs = [pl.BlockSpec((1, T, BW), lambda b, s: (b, 0, 0)), st_c, st_m]
    attn = attn_args is not None
    band = False
    if attn:
        band = attn_args[-1]
        a_specs, a_args = _attn_specs(*attn_args)
        in_specs += a_specs
        args += a_args
        out_shape.append(jax.ShapeDtypeStruct((B, T, BW), BF16))
        out_specs.append(pl.BlockSpec((1, rows, BW), lambda b, s: (b, s, 0)))
    return pl.pallas_call(
        functools.partial(_mixer_kernel, attn=attn, band=band),
        out_shape=out_shape,
        grid=(B, nc),
        in_specs=in_specs,
        out_specs=out_specs,
        scratch_shapes=[pltpu.VMEM((2, M_HEADS, M_HDIM, 2 * M_HDIM), F32),
                        pltpu.VMEM((2, 8, LANES), F32)],
        compiler_params=_cparams(2),
        name="mixers" if attn else "mlstm",
    )(*args)


def _merge_kernel(x_ref, sh_ref, sc_ref, gt_ref, a_ref, ap_ref, an_ref, ic_ref, hm_ref, om_ref, at_ref,
                  wg_ref, pw_ref, ps_ref, nw_ref, wb_ref, wo_ref, lg_ref, lb_ref,
                  o_ref, abuf, *, alpha):
    i = pl.program_id(1)
    nt = pl.num_programs(1)
    tm = x_ref.shape[1]
    D = x_ref.shape[2]
    rows = min(tm, MERGE_SLAB)
    abuf[0:HALO, :] = jnp.where(i > 0, ap_ref[0].astype(F32), 0.0)
    abuf[HALO:HALO + tm, :] = a_ref[0].astype(F32)
    abuf[HALO + tm:, :] = jnp.where(i < nt - 1, an_ref[0].astype(F32), 0.0)

    def slab(r0):
        rs = slice(r0, r0 + rows)
        x = x_ref[0, rs, :]
        h = _modulated(x, sh_ref[0], sc_ref[0]).astype(BF16)
        yield
        zh = [_dot(h, wg_ref[:, bi * D:(bi + 1) * D]) for bi in range(N_BRANCH)]
        br_attn = _dot(at_ref[0, rs, :], wb_ref[2])
        yield
        pooled = []
        for gi, w in enumerate(POOL_WINDOWS):
            cs = slice(gi * POOL_GDIM, (gi + 1) * POOL_GDIM)
            lo = HALO + r0 - w // 2
            acc = abuf[lo:lo + rows, cs]
            for k in range(1, w):
                acc = acc + abuf[lo + k:lo + k + rows, cs]
            u = acc * ic_ref[gi, rs, :] - abuf[HALO + r0:HALO + r0 + rows, cs]
            pooled.append(_dot(u.astype(BF16), pw_ref[gi]))
        pool = (jnp.concatenate(pooled, axis=1) * ps_ref[...]).astype(BF16)
        br_pool = _dot(pool, wb_ref[0])
        heads = [_layer_norm(hm_ref[0, rs, hd * M_HDIM:(hd + 1) * M_HDIM]) for hd in range(M_HEADS)]
        ml = (jnp.concatenate(heads, axis=1) * nw_ref[...] * _sigmoid(om_ref[0, rs, :].astype(F32))).astype(BF16)
        br_ml = _dot(ml, wb_ref[1])
        yield
        y2 = None
        for zi, bi in zip(zh, (br_pool, br_ml, br_attn)):
            yi = (jnp.tanh(zi) + 1.0) * bi
            y2 = yi if y2 is None else y2 + yi
        z2 = _dot(y2.astype(BF16), wo_ref[...])
        yield
        o_ref[0, rs, :] = _layer_norm(alpha * x + (0.5 * gt_ref[0]) * z2) * lg_ref[...] + lb_ref[...]
        yield

    slabs = [slab(r0) for r0 in range(0, tm, rows)]
    for _ in range(5):
        for sb in slabs:
            next(sb)


def _pool_inv_counts(T):
    t = jnp.arange(T)
    rows = []
    for w in POOL_WINDOWS:
        cnt = jnp.clip(t - w // 2 + w, 0, T) - jnp.clip(t - w // 2, 0, T)
        rows.append(1.0 / cnt.astype(F32))
    return jnp.broadcast_to(jnp.stack(rows)[:, :, None], (len(POOL_WINDOWS), T, LANES))


def _merge(x, shift, scale, gate, a, hm, om, at, wg, pw, ps, nw, wb, wo, lg, lb, tm, alpha):
    B, T, D = x.shape
    per_b = shift.shape[0] > 1
    mod_map = (lambda b, i: (b, 0, 0)) if per_b else (lambda b, i: (0, 0, 0))
    tok = lambda n: pl.BlockSpec((1, tm, n), lambda b, i: (b, i, 0))
    full = lambda arr: pl.BlockSpec(arr.shape, lambda b, i: (0,) * arr.ndim)
    r = tm // HALO
    nh = T // HALO
    prev = pl.BlockSpec((1, HALO, BW), lambda b, i: (b, jnp.maximum(i * r - 1, 0), 0))
    nxt = pl.BlockSpec((1, HALO, BW), lambda b, i: (b, jnp.minimum((i + 1) * r, nh - 1), 0))
    mod = pl.BlockSpec((1, 1, D), mod_map)
    return pl.pallas_call(
        functools.partial(_merge_kernel, alpha=alpha),
        out_shape=jax.ShapeDtypeStruct((B, T, D), F32),
        grid=(B, T // tm),
        in_specs=[tok(D), mod, mod, mod, tok(BW), prev, nxt,
                  pl.BlockSpec((len(POOL_WINDOWS), tm, LANES), lambda b, i: (0, i, 0)),
                  tok(BW), tok(BW), tok(BW),
                  full(wg), full(pw), full(ps), full(nw), full(wb), full(wo), full(lg), full(lb)],
        out_specs=tok(D),
        scratch_shapes=[pltpu.VMEM((tm + 2 * HALO, BW), F32)],
        compiler_params=_cparams(2),
        name="merge",
    )(x, shift, scale, gate, a, a, a, _pool_inv_counts(T), hm, om, at, wg, pw, ps, nw, wb, wo, lg, lb)


def _ffn_kernel(x_ref, sh_ref, sc_ref, gt_ref, wg_ref, wu_ref, wd_ref, lg_ref, lb_ref, o_ref, *, alpha, fc):
    tm = x_ref.shape[1]
    rows = min(tm, FFN_SLAB)
    dff = wg_ref.shape[1]
    n = dff // fc

    def slab(r0):
        rs = slice(r0, r0 + rows)
        x = x_ref[0, rs, :]
        h = _modulated(x, sh_ref[0], sc_ref[0]).astype(BF16)
        yield

        def gate_up(j):
            cs = slice(j * fc, (j + 1) * fc)
            return _dot(h, wg_ref[:, cs]), _dot(h, wu_ref[:, cs])

        acc = None
        nxt = gate_up(0)
        for j in range(n):
            hg, up = nxt
            if j + 1 < n:
                nxt = gate_up(j + 1)
            act = (hg * (jnp.tanh(hg) + 1.0) * up).astype(BF16)
            part = _dot(act, wd_ref[j * fc:(j + 1) * fc, :])
            acc = part if acc is None else acc + part
            yield
        o_ref[0, rs, :] = _layer_norm(alpha * x + gt_ref[0] * acc) * lg_ref[...] + lb_ref[...]
        yield

    slabs = [slab(r0) for r0 in range(0, tm, rows)]
    for _ in range(n + 2):
        for sb in slabs:
            next(sb)


def _ffn(x, shift, scale, gate, wg, wu, wd, lg, lb, tm, alpha):
    B, T, D = x.shape
    per_b = shift.shape[0] > 1
    mod_map = (lambda b, i: (b, 0, 0)) if per_b else (lambda b, i: (0, 0, 0))
    tok = pl.BlockSpec((1, tm, D), lambda b, i: (b, i, 0))
    full = lambda arr: pl.BlockSpec(arr.shape, lambda b, i: (0,) * arr.ndim, pipeline_mode=pl.Buffered(1))
    mod = pl.BlockSpec((1, 1, D), mod_map)
    dff = wg.shape[1]
    fc = 2 * LANES if dff % (2 * LANES) == 0 else dff
    return pl.pallas_call(
        functools.partial(_ffn_kernel, alpha=alpha, fc=fc),
        out_shape=jax.ShapeDtypeStruct((B, T, D), F32),
        grid=(B, T // tm),
        in_specs=[tok, mod, mod, mod, full(wg), full(wu), full(wd), full(lg), full(lb)],
        out_specs=tok,
        compiler_params=_cparams(2),
        name="ffn",
    )(x, shift, scale, gate, wg, wu, wd, lg, lb)


def _rope_tables(T):
    t = jnp.arange(T)
    row = (t // GRID_W).astype(F32)
    col = (t % GRID_W).astype(F32)
    inv = ROPE_BASE ** (-jnp.arange(ROPE_PAIRS, dtype=F32) / ROPE_PAIRS)
    ang_r = row[:, None] * inv
    ang_c = col[:, None] * inv
    ang = jnp.concatenate([ang_r, ang_r, ang_c, ang_c], axis=-1)
    sign = jnp.concatenate([-jnp.ones(ROPE_PAIRS), jnp.ones(ROPE_PAIRS)] * 2).astype(F32)
    cos = jnp.tile(jnp.cos(ang), (1, 2))
    sin = jnp.tile(jnp.sin(ang) * sign, (1, 2))
    return cos, sin


def _pack_w_in(w_in_l):
    offs = np.cumsum([BW, BW, BW, BW, BW, 2 * M_HEADS, 2 * M_HEADS, BW, 2 * A_HDIM, 2 * A_HDIM])
    a, qm, km, vm, om, ig, fg, qa, ka, va, gates = jnp.split(w_in_l, offs.tolist(), axis=1)
    w_main = jnp.concatenate([a, qm, vm, om, qa, ka, va], axis=1).astype(BF16)
    h = M_HEADS
    w_gate_t = jnp.concatenate([ig[:, :h], fg[:, :h], ig[:, h:], fg[:, h:]], axis=1).T.astype(BF16)
    return w_main, km.T.astype(BF16), w_gate_t, (0.5 * gates).astype(BF16)


def kernel(x, c, ctx, c_ctx, w_mod, b_mod, w_in, pool_w, pool_scale, m_b_i, m_b_f, m_norm_w, attn_sink,
           w_branch, w_out, ln1_g, ln1_b, ln2_g, ln2_b, w_ffn_gate, w_ffn_up, w_ffn_down):
    B, T, D = x.shape
    C = ctx.shape[1]
    depth = w_mod.shape[0]
    alpha = (2 * depth) ** 0.25
    tm_x = 512 if T % 512 == 0 else T
    tm_f = 1024 if T % 1024 == 0 else tm_x
    tm_c = C
    h = M_HEADS

    rows = ((B + 1 + 7) // 8) * 8
    cc = jnp.zeros((rows, D), F32).at[:B].set(c).at[B].set(c_ctx)
    mods = _modulation(cc, w_mod, b_mod)
    cos_x, sin_x = _rope_tables(T)
    cos_c, sin_c = jnp.ones((C, LANES), F32), jnp.zeros((C, LANES), F32)
    zero_c = jnp.zeros((B, 2, h, M_HDIM, 2 * M_HDIM), F32)
    zero_m = jnp.zeros((B, 2, 8, LANES), F32)

    for l in range(depth):
        last = l == depth - 1
        mx = [mods[l, :B, j * D:(j + 1) * D].reshape(B, 1, D) for j in range(6)]
        mc = [mods[l, B:B + 1, j * D:(j + 1) * D].reshape(1, 1, D) for j in range(6)]
        w_main, w_kt, w_gate_t, w_bgate = _pack_w_in(w_in[l])
        bi, bf = m_b_i[l], m_b_f[l]
        bg = jnp.concatenate([bi[0], bf[0], bi[1], bf[1]]).reshape(16, 1)
        row = lambda v: v.reshape(1, -1)

        px = _project(x, mx[0], mx[1], cos_x, sin_x, w_main, w_kt, w_gate_t, bg, tm_f)
        pc = _project(ctx, mc[0], mc[1], cos_c, sin_c, w_main, w_kt, w_gate_t, bg, tm_c)
        a_x, qm_x, kt_x, vm_x, om_x, qa_x, ka_x, va_x, g_x = px
        a_c, qm_c, kt_c, vm_c, om_c, qa_c, ka_c, va_c, g_c = pc

        ctx_attn = None if last else (qa_c, ka_c, va_c, ka_c, va_c, attn_sink[l], False)
        mix_c = _mixers(qm_c, kt_c, vm_c, g_c, zero_c, zero_m, ctx_attn)
        h_c, st_c, st_m = mix_c[:3]
        h_x, _, _, at_x = _mixers(qm_x, kt_x, vm_x, g_x, st_c, st_m,
                                  (qa_x, ka_x, va_x, ka_c, va_c, attn_sink[l], True))

        wargs = (w_bgate, pool_w[l].astype(BF16), row(pool_scale[l]), row(m_norm_w[l]),
                 w_branch[l].astype(BF16), w_out[l].astype(BF16), row(ln1_g[l]), row(ln1_b[l]))
        fargs = ((0.5 * w_ffn_gate[l]).astype(BF16), w_ffn_up[l].astype(BF16), w_ffn_down[l].astype(BF16),
                 row(ln2_g[l]), row(ln2_b[l]))
        x1 = _merge(x, mx[0], mx[1], mx[2], a_x, h_x, om_x, at_x, *wargs, tm_x, alpha)
        x_new = _ffn(x1, mx[3], mx[4], mx[5], *fargs, tm_f, alpha)
        if not last:
            at_c = mix_c[3]
            c1 = _merge(ctx, mc[0], mc[1], mc[2], a_c, h_c, om_c, at_c, *wargs, tm_c, alpha)
            ctx = _ffn(c1, mc[3], mc[4], mc[5], *fargs, tm_c, alpha)
        x = x_new
    return x
```

```python
import functools

import jax
import jax.numpy as jnp
import numpy as np
from jax import lax
from jax.experimental import pallas as pl
from jax.experimental.pallas import tpu as pltpu

F32 = jnp.float32
BF16 = jnp.bfloat16

GRID_W = 64
N_BRANCH = 3
POOL_WINDOWS = (2, 4, 8, 16)
POOL_GDIM = 128
M_HEADS = 4
M_HDIM = 128
M_CHUNK = 128
A_HDIM = 64
A_Q_HEADS = 8
A_KV_HEADS = 2
A_GROUP = 4
WINDOW = 128
A_BLOCK = 128
ROPE_BASE = 10000.0
ROPE_PAIRS = 16
BW = 512
LN_EPS = 1e-5
NEG = -1e30
LOG2E = 1.4426950408889634
LANES = 128
HALO = 16
PROJ_SLAB = 512
MIX_ROWS = 512
MERGE_SLAB = 256
FFN_SLAB = 512
VMEM_LIMIT_BYTES = 56 * 1024 * 1024


def _cparams(n_axes):
    return pltpu.CompilerParams(dimension_semantics=("arbitrary",) * n_axes,
                                vmem_limit_bytes=VMEM_LIMIT_BYTES)


def _dot(a, b):
    return jnp.dot(a, b, preferred_element_type=F32)


def _dot_nt(a, b):
    return lax.dot_general(a, b, (((1,), (1,)), ((), ())), preferred_element_type=F32)


def _dot_tn(a, b):
    return lax.dot_general(a, b, (((0,), (0,)), ((), ())), preferred_element_type=F32)


def _layer_norm(x):
    mu = jnp.mean(x, axis=-1, keepdims=True)
    xc = x - mu
    var = jnp.mean(xc * xc, axis=-1, keepdims=True)
    return xc * lax.rsqrt(var + LN_EPS)


def _modulated(x, shift, scale):
    return _layer_norm(x) * (1.0 + scale) + shift


def _sigmoid(x):
    return 0.5 * jnp.tanh(0.5 * x) + 0.5


def _mod_kernel(c_ref, w_ref, b_ref, o_ref):
    c = c_ref[...]
    s = (c * _sigmoid(c)).astype(BF16)
    o_ref[0] = _dot(s, w_ref[0].astype(BF16)) + b_ref[0]


def _modulation(cc, w_mod, b_mod):
    L, D, D6 = w_mod.shape
    R = cc.shape[0]
    tn = D
    return pl.pallas_call(
        _mod_kernel,
        out_shape=jax.ShapeDtypeStruct((L, R, D6), F32),
        grid=(L, D6 // tn),
        in_specs=[pl.BlockSpec((R, D), lambda l, j: (0, 0)),
                  pl.BlockSpec((1, D, tn), lambda l, j: (l, 0, j)),
                  pl.BlockSpec((1, 1, tn), lambda l, j: (l, 0, j))],
        out_specs=pl.BlockSpec((1, R, tn), lambda l, j: (l, 0, j)),
        compiler_params=_cparams(2),
        name="mod",
    )(cc, w_mod, b_mod.reshape(L, 1, D6))


def _swap16(x):
    lane = lax.broadcasted_iota(jnp.int32, x.shape, 1)
    up = pltpu.roll(x, LANES - ROPE_PAIRS, 1)
    dn = pltpu.roll(x, ROPE_PAIRS, 1)
    return jnp.where((lane & ROPE_PAIRS) == 0, up, dn)


def _rope(x, cos, sin):
    outs = []
    for j in range(x.shape[1] // LANES):
        xs = x[:, j * LANES:(j + 1) * LANES]
        outs.append(xs * cos + _swap16(xs) * sin)
    return outs[0] if len(outs) == 1 else jnp.concatenate(outs, axis=1)


def _dup_heads(x):
    lane = lax.broadcasted_iota(jnp.int32, x.shape, 1)
    sw = pltpu.roll(x, A_HDIM, 1)
    low = lane < A_HDIM
    return jnp.concatenate([jnp.where(low, x, sw), jnp.where(low, sw, x)], axis=1)


def _chunk_scan(v, reverse):
    lane = lax.broadcasted_iota(jnp.int32, v.shape, 1)
    sh = 1
    while sh < LANES:
        if reverse:
            v = v + jnp.where(lane < LANES - sh, pltpu.roll(v, LANES - sh, 1), 0.0)
        else:
            v = v + jnp.where(lane >= sh, pltpu.roll(v, sh, 1), 0.0)
        sh *= 2
    return v


def _proj_kernel(x_ref, sh_ref, sc_ref, cos_ref, sin_ref, w_ref, wk_ref, wg_ref, bg_ref,
                 a_ref, qm_ref, kt_ref, vm_ref, om_ref, qa_ref, ka_ref, va_ref, g_ref):
    tm = x_ref.shape[1]
    rows = min(tm, PROJ_SLAB)

    def slab(r0):
        rs = slice(r0, r0 + rows)
        h = _modulated(x_ref[0, rs, :], sh_ref[0], sc_ref[0]).astype(BF16)
        yield

        def cols(j, n=BW):
            return _dot(h, w_ref[:, j:j + n])

        a_ref[0, rs, :] = cols(0).astype(a_ref.dtype)
        qm_ref[0, rs, :] = cols(BW).astype(qm_ref.dtype)
        vm_ref[0, rs, :] = cols(2 * BW).astype(vm_ref.dtype)
        om_ref[0, rs, :] = cols(3 * BW).astype(om_ref.dtype)
        kt_ref[0, :, rs] = _dot_nt(wk_ref[...], h).astype(kt_ref.dtype)
        qa = cols(4 * BW)
        kv = cols(5 * BW, 2 * LANES)
        z = _dot_nt(wg_ref[...], h) + bg_ref[...]
        yield
        cos = cos_ref[rs, :]
        sin = sin_ref[rs, :]
        qa = _rope(qa, cos, sin) * (A_HDIM ** -0.5 * LOG2E)
        qa_ref[0, rs, :] = qa.astype(qa_ref.dtype)
        ka_ref[0, rs, :] = _dup_heads(_rope(kv[:, :LANES], cos, sin)).astype(ka_ref.dtype)
        va_ref[0, rs, :] = _dup_heads(kv[:, LANES:]).astype(va_ref.dtype)
        row = lax.broadcasted_iota(jnp.int32, (16, LANES), 0)
        is_f = (row & 4) != 0
        is_bwd = row >= 8
        for c in range(rows // LANES):
            zc = z[:, c * LANES:(c + 1) * LANES]
            lf = jnp.minimum(zc, 0.0) - jnp.log1p(jnp.exp(-jnp.abs(zc)))
            cum = jnp.where(is_bwd, _chunk_scan(lf, True), _chunk_scan(lf, False))
            out = jnp.where(is_f, cum, zc)
            g_ref[0, 0, :, r0 + c * LANES:r0 + (c + 1) * LANES] = out[:8]
            g_ref[0, 1, :, r0 + c * LANES:r0 + (c + 1) * LANES] = out[8:]
        yield

    slabs = [slab(r0) for r0 in range(0, tm, rows)]
    for _ in range(3):
        for sb in slabs:
            next(sb)


def _project(x, shift, scale, cos, sin, w, wk, wg, bg, tm):
    B, T, D = x.shape
    nw = w.shape[1]
    per_b = shift.shape[0] > 1
    mod_map = (lambda b, i: (b, 0, 0)) if per_b else (lambda b, i: (0, 0, 0))
    tok = lambda n: pl.BlockSpec((1, tm, n), lambda b, i: (b, i, 0))
    tshape = lambda n: jax.ShapeDtypeStruct((B, T, n), BF16)
    outs = [tshape(BW), tshape(BW), jax.ShapeDtypeStruct((B, BW, T), BF16), tshape(BW), tshape(BW), tshape(BW),
            tshape(2 * LANES), tshape(2 * LANES), jax.ShapeDtypeStruct((B, 2, 8, T), F32)]
    return pl.pallas_call(
        _proj_kernel,
        out_shape=outs,
        grid=(B, T // tm),
        in_specs=[tok(D),
                  pl.BlockSpec((1, 1, D), mod_map),
                  pl.BlockSpec((1, 1, D), mod_map),
                  pl.BlockSpec((tm, LANES), lambda b, i: (i, 0)),
                  pl.BlockSpec((tm, LANES), lambda b, i: (i, 0)),
                  pl.BlockSpec((D, nw), lambda b, i: (0, 0)),
                  pl.BlockSpec((BW, D), lambda b, i: (0, 0)),
                  pl.BlockSpec((16, D), lambda b, i: (0, 0)),
                  pl.BlockSpec((16, 1), lambda b, i: (0, 0))],
        out_specs=[tok(BW), tok(BW), pl.BlockSpec((1, BW, tm), lambda b, i: (b, 0, i)), tok(BW), tok(BW), tok(BW),
                   tok(2 * LANES), tok(2 * LANES),
                   pl.BlockSpec((1, 2, 8, tm), lambda b, i: (b, 0, 0, i))],
        compiler_params=_cparams(2),
        name="proj",
    )(x, shift, scale, cos, sin, w, wk, wg, bg)


def _split3(x):
    hi = x.astype(BF16).astype(F32)
    r1 = x - hi
    mid = r1.astype(BF16).astype(F32)
    return hi, mid, r1 - mid


def _mlstm_chain(rev, hd, n, q_ref, kt_ref, v_ref, g_ref, c_s, m_s, h_ref, row0):
    L = M_CHUNK
    dirn = 1 if rev else 0
    sl = slice(hd * M_HDIM, (hd + 1) * M_HDIM)
    kscale = M_HDIM ** -0.5
    rid = lax.broadcasted_iota(jnp.int32, (16, L), 0)
    m_prev = m_s[dirn, hd:hd + 1, :]
    c_prev = c_s[dirn, hd]
    items = []
    for ci in (reversed(range(n)) if rev else range(n)):
        rs = slice(ci * L, (ci + 1) * L)
        qh = q_ref[0, rs, sl]
        kt = kt_ref[0, sl, rs]
        vaug = jnp.concatenate([v_ref[0, rs, sl], jnp.ones((L, LANES), BF16)], axis=1)
        ig_row = g_ref[0, 0, hd:hd + 1, rs]
        b_row = g_ref[0, 0, 4 + hd:5 + hd, rs]
        b3 = _split3(b_row)
        r3 = _split3(ig_row - b_row)
        lhs = jnp.where(rid == 0, b3[0], jnp.where(rid == 1, b3[1], jnp.where(rid == 2, b3[2],
                        jnp.where(rid < 6, 1.0, 0.0))))
        rhs_l = jnp.where(rid < 3, 1.0, jnp.where(rid == 3, r3[0], jnp.where(rid == 4, r3[1],
                          jnp.where(rid == 5, r3[2], 0.0))))
        rhs_r = jnp.where(rid < 3, 1.0, 0.0)
        db = _dot_tn(lhs.astype(BF16), jnp.concatenate([rhs_l, rhs_r], axis=1).astype(BF16))
        qk = _dot(qh, kt)
        b_last = b_row[:, 0:1] if rev else b_row[:, L - 1:L]
        g_row = b_last - b_row + ig_row
        m_new = jnp.maximum(b_last + m_prev, jnp.max(g_row, axis=-1, keepdims=True))
        wk = jnp.exp(g_row - m_new) * kscale
        decay = jnp.exp(b_last + m_prev - m_new)
        upd = _dot((kt.astype(F32) * wk).astype(BF16), vaug)
        rhs = jnp.concatenate([vaug, c_prev.astype(BF16)], axis=0)
        items.append(dict(ci=ci, qh=qh, rhs=rhs, db=db, qk=qk, m_prev=m_prev))
        c_prev = jnp.concatenate([decay, decay], axis=1) * c_prev + upd
        m_prev = m_new
    c_s[dirn, hd] = c_prev
    m_s[dirn, hd:hd + 1, :] = m_prev
    yield
    rr = lax.broadcasted_iota(jnp.int32, (L, L), 0)
    cc = lax.broadcasted_iota(jnp.int32, (L, L), 1)
    causal = (cc >= rr) if rev else (cc <= rr)
    for it in items:
        log_d = jnp.where(causal, it["db"][:, :L], NEG)
        inter = it["db"][:, L:] + it["m_prev"]
        m_row = jnp.maximum(inter, jnp.max(log_d, axis=-1, keepdims=True))
        p = (it["qk"] * (jnp.exp(log_d - m_row) * kscale)).astype(BF16)
        qw = (it["qh"].astype(F32) * jnp.exp(inter - m_row)).astype(BF16)
        it["lhs"] = jnp.concatenate([p, qw], axis=1)
        it["m_row"] = m_row
    yield
    for it in items:
        it["tot"] = _dot(it["lhs"], it["rhs"])
    yield
    for it in items:
        tot = it["tot"]
        den = jnp.maximum(jnp.abs(tot[:, M_HDIM:]), jnp.exp(-it["m_row"]))
        h_ref[0, pl.ds(row0 + it["ci"] * L, L), sl] += tot[:, :M_HDIM] / den
    yield


def _attn_groups(sink_ref, q_ref, kc_ref, vc_ref, band_refs, o_ref, i, nsteps):
    band = band_refs is not None
    if band:
        kp_ref, kb_ref, kn_ref, vp_ref, vb_ref, vn_ref = band_refs
    Q = A_BLOCK
    ns = q_ref.shape[1] // Q
    NP = A_GROUP // 2
    if band:
        rr = lax.broadcasted_iota(jnp.int32, (NP * Q, Q), 0) & (Q - 1)
        cc = lax.broadcasted_iota(jnp.int32, (NP * Q, Q), 1)
        inner_prev = jnp.where(cc >= rr, 0.0, NEG)
        inner_next = jnp.where(cc <= rr, 0.0, NEG)
        edge_prev = jnp.where((cc >= rr) & (i > 0), 0.0, NEG)
        edge_next = jnp.where((cc <= rr) & (i < nsteps - 1), 0.0, NEG)
    out_cols = [[None] * (A_Q_HEADS // 2) for _ in range(ns)]

    def head_group(j, hk):
        ks = slice(hk * LANES, (hk + 1) * LANES)
        qr = slice(j * Q, (j + 1) * Q)
        if band:
            blk = lambda ref, jj: ref[0, jj * Q:(jj + 1) * Q, ks]
            kprev, vprev = ((kp_ref[0, :, ks], vp_ref[0, :, ks]) if j == 0
                            else (blk(kb_ref, j - 1), blk(vb_ref, j - 1)))
            knext, vnext = ((kn_ref[0, :, ks], vn_ref[0, :, ks]) if j == ns - 1
                            else (blk(kb_ref, j + 1), blk(vb_ref, j + 1)))
            kcat = jnp.concatenate([kc_ref[0, :, ks], blk(kb_ref, j), kprev, knext], axis=0)
            vcat = jnp.concatenate([vc_ref[0, :, ks], blk(vb_ref, j), vprev, vnext], axis=0)
            bias_prev = edge_prev if j == 0 else inner_prev
            bias_next = edge_next if j == ns - 1 else inner_next
        else:
            kcat = kc_ref[0, :, ks]
            vcat = vc_ref[0, :, ks]
        nk = kcat.shape[0]
        lane = lax.broadcasted_iota(jnp.int32, (nk, LANES), 1)
        m_lo = jnp.where(lane < A_HDIM, 1.0, 0.0).astype(BF16)
        m_hi = jnp.where(lane < A_HDIM, 0.0, 1.0).astype(BF16)
        k_bd = jnp.concatenate([kcat * m_lo, kcat * m_hi], axis=0)
        v_bd = jnp.concatenate([jnp.concatenate([vcat * m_lo, m_lo], axis=1),
                                jnp.concatenate([vcat * m_hi, m_hi], axis=1)], axis=0)
        qstack = jnp.concatenate(
            [q_ref[0, qr, (hk * NP + pr) * LANES:(hk * NP + pr + 1) * LANES] for pr in range(NP)], axis=0)
        sc = _dot_nt(qstack, k_bd)
        sinks = [jnp.concatenate([jnp.full((Q, LANES), sink_ref[hk, 2 * pr + par] * LOG2E, F32)
                                  for pr in range(NP)], axis=0) for par in range(2)]
        yield
        ps, ms = [], []
        for par in range(2):
            cols = [sc[:, par * nk + c * LANES:par * nk + (c + 1) * LANES] for c in range(nk // LANES)]
            if band:
                cols[-2] = cols[-2] + bias_prev
                cols[-1] = cols[-1] + bias_next
            part = cols[0]
            for cj in cols[1:]:
                part = jnp.maximum(part, cj)
            m = jnp.maximum(jnp.max(part, axis=-1, keepdims=True), sinks[par])
            ps += [jnp.exp2(cj - m).astype(BF16) for cj in cols]
            ms.append(m)
        p = jnp.concatenate(ps, axis=1)
        yield
        o = _dot(p, v_bd)
        yield
        lane_q = lax.broadcasted_iota(jnp.int32, (NP * Q, LANES), 1)
        sink_term = jnp.where(lane_q < A_HDIM, jnp.exp2(sinks[0] - ms[0]), jnp.exp2(sinks[1] - ms[1]))
        o = o[:, :LANES] / (o[:, LANES:] + sink_term)
        for pr in range(NP):
            out_cols[j][hk * NP + pr] = o[pr * Q:(pr + 1) * Q]
        yield

    def finalize():
        for j in range(ns):
            o_ref[0, j * Q:(j + 1) * Q, :] = jnp.concatenate(out_cols[j], axis=1).astype(o_ref.dtype)

    return [head_group(j, hk) for j in range(ns) for hk in range(A_KV_HEADS)], finalize


def _attn_specs(q, k, v, kc, vc, sink, band):
    T = q.shape[1]
    C = kc.shape[1]
    nb = T // A_BLOCK
    rows = min(MIX_ROWS, T)
    ns = rows // A_BLOCK
    kvw = 2 * LANES
    in_specs = [pl.BlockSpec(memory_space=pltpu.SMEM),
                pl.BlockSpec((1, rows, BW), lambda b, i: (b, i, 0)),
                pl.BlockSpec((1, C, kvw), lambda b, i: (b, 0, 0)),
                pl.BlockSpec((1, C, kvw), lambda b, i: (b, 0, 0))]
    args = [sink, q, kc, vc]
    if band:
        prev = pl.BlockSpec((1, A_BLOCK, kvw), lambda b, i: (b, jnp.maximum(ns * i - 1, 0), 0))
        cur = pl.BlockSpec((1, rows, kvw), lambda b, i: (b, i, 0))
        nxt = pl.BlockSpec((1, A_BLOCK, kvw), lambda b, i: (b, jnp.minimum(ns * (i + 1), nb - 1), 0))
        in_specs += [prev, cur, nxt, prev, cur, nxt]
        args += [k, k, k, v, v, v]
    return in_specs, args


def _mixer_kernel(*refs, attn, band):
    qf_ref, qb_ref, ktf_ref, ktb_ref, vf_ref, vb_ref, gf_ref, gb_ref, c0_ref, m0_ref = refs[:10]
    k = 10
    if attn:
        sink_ref, q_ref, kc_ref, vc_ref = refs[k:k + 4]
        k += 4
        band_refs = refs[k:k + 6] if band else None
        k += 6 if band else 0
    h_ref, cn_ref, mn_ref = refs[k:k + 3]
    k += 3
    if attn:
        o_ref = refs[k]
        k += 1
    c_s, m_s = refs[k:k + 2]

    s = pl.program_id(1)
    nc = pl.num_programs(1)
    rows = qf_ref.shape[1]
    n = rows // M_CHUNK

    @pl.when(s == 0)
    def _():
        c_s[...] = c0_ref[0]
        m_s[...] = m0_ref[0]
        h_ref[...] = jnp.zeros_like(h_ref)

    row_f = pl.multiple_of(s * rows, rows)
    row_b = pl.multiple_of((nc - 1 - s) * rows, rows)
    gens = []
    for hd in range(M_HEADS):
        gens.append(_mlstm_chain(False, hd, n, qf_ref, ktf_ref, vf_ref, gf_ref, c_s, m_s, h_ref, row_f))
        gens.append(_mlstm_chain(True, hd, n, qb_ref, ktb_ref, vb_ref, gb_ref, c_s, m_s, h_ref, row_b))
    if attn:
        groups, finalize = _attn_groups(sink_ref, q_ref, kc_ref, vc_ref, band_refs, o_ref, s, nc)
        step = len(gens) // len(groups)
        for gi, gr in enumerate(groups):
            gens.insert(gi * (step + 1) + step, gr)
    for _ in range(4):
        for ge in gens:
            next(ge)
    if attn:
        finalize()

    @pl.when(s == nc - 1)
    def _():
        cn_ref[0] = c_s[...]
        mn_ref[0] = m_s[...]


def _mixers(qm, kmt, vm, g, c0, m0, attn_args=None):
    B, T, _ = qm.shape
    rows = min(MIX_ROWS, T)
    nc = T // rows
    st_c = pl.BlockSpec((1, 2, M_HEADS, M_HDIM, 2 * M_HDIM), lambda b, s: (b, 0, 0, 0, 0))
    st_m = pl.BlockSpec((1, 2, 8, LANES), lambda b, s: (b, 0, 0, 0))
    tok_f = pl.BlockSpec((1, rows, BW), lambda b, s: (b, s, 0))
    tok_b = pl.BlockSpec((1, rows, BW), lambda b, s: (b, nc - 1 - s, 0))
    kt_f = pl.BlockSpec((1, BW, rows), lambda b, s: (b, 0, s))
    kt_b = pl.BlockSpec((1, BW, rows), lambda b, s: (b, 0, nc - 1 - s))
    g_f = pl.BlockSpec((1, 1, 8, rows), lambda b, s: (b, 0, 0, s))
    g_b = pl.BlockSpec((1, 1, 8, rows), lambda b, s: (b, 1, 0, nc - 1 - s))
    in_specs = [tok_f, tok_b, kt_f, kt_b, tok_f, tok_b, g_f, g_b, st_c, st_m]
    args = [qm, qm, kmt, kmt, vm, vm, g, g, c0, m0]
    out_shape = [jax.ShapeDtypeStruct((B, T, BW), F32),
                 jax.ShapeDtypeStruct(c0.shape, F32),
                 jax.ShapeDtypeStruct(m0.shape, F32)]
    out_specs = [pl.BlockSpec((1, T, BW), lambda b, s: (b, 0, 0)), st_c, st_m]
    attn = attn_args is not None
    band = False
    if attn:
        band = attn_args[-1]
        a_specs, a_args = _attn_specs(*attn_args)
        in_specs += a_specs
        args += a_args
        out_shape.append(jax.ShapeDtypeStruct((B, T, BW), BF16))
        out_specs.append(pl.BlockSpec((1, rows, BW), lambda b, s: (b, s, 0)))
    return pl.pallas_call(
        functools.partial(_mixer_kernel, attn=attn, band=band),
        out_shape=out_shape,
        grid=(B, nc),
        in_specs=in_specs,
        out_specs=out_specs,
        scratch_shapes=[pltpu.VMEM((2, M_HEADS, M_HDIM, 2 * M_HDIM), F32),
                        pltpu.VMEM((2, 8, LANES), F32)],
        compiler_params=_cparams(2),
        name="mixers" if attn else "mlstm",
    )(*args)


def _merge_kernel(x_ref, sh_ref, sc_ref, gt_ref, a_ref, ap_ref, an_ref, ic_ref, hm_ref, om_ref, at_ref,
                  wg_ref, pw_ref, ps_ref, nw_ref, wb_ref, wo_ref, lg_ref, lb_ref,
                  o_ref, abuf, *, alpha):
    i = pl.program_id(1)
    nt = pl.num_programs(1)
    tm = x_ref.shape[1]
    D = x_ref.shape[2]
    rows = min(tm, MERGE_SLAB)
    abuf[0:HALO, :] = jnp.where(i > 0, ap_ref[0].astype(F32), 0.0)
    abuf[HALO:HALO + tm, :] = a_ref[0].astype(F32)
    abuf[HALO + tm:, :] = jnp.where(i < nt - 1, an_ref[0].astype(F32), 0.0)

    def slab(r0):
        rs = slice(r0, r0 + rows)
        x = x_ref[0, rs, :]
        h = _modulated(x, sh_ref[0], sc_ref[0]).astype(BF16)
        yield
        zh = [_dot(h, wg_ref[:, bi * D:(bi + 1) * D]) for bi in range(N_BRANCH)]
        br_attn = _dot(at_ref[0, rs, :], wb_ref[2])
        yield
        pooled = []
        for gi, w in enumerate(POOL_WINDOWS):
            cs = slice(gi * POOL_GDIM, (gi + 1) * POOL_GDIM)
            lo = HALO + r0 - w // 2
            acc = abuf[lo:lo + rows, cs]
            for k in range(1, w):
                acc = acc + abuf[lo + k:lo + k + rows, cs]
            u = acc * ic_ref[gi, rs, :] - abuf[HALO + r0:HALO + r0 + rows, cs]
            pooled.append(_dot(u.astype(BF16), pw_ref[gi]))
        pool = (jnp.concatenate(pooled, axis=1) * ps_ref[...]).astype(BF16)
        br_pool = _dot(pool, wb_ref[0])
        heads = [_layer_norm(hm_ref[0, rs, hd * M_HDIM:(hd + 1) * M_HDIM]) for hd in range(M_HEADS)]
        ml = (jnp.concatenate(heads, axis=1) * nw_ref[...] * _sigmoid(om_ref[0, rs, :].astype(F32))).astype(BF16)
        br_ml = _dot(ml, wb_ref[1])
        yield
        y2 = None
        for zi, bi in zip(zh, (br_pool, br_ml, br_attn)):
            yi = (jnp.tanh(zi) + 1.0) * bi
            y2 = yi if y2 is None else y2 + yi
        z2 = _dot(y2.astype(BF16), wo_ref[...])
        yield
        o_ref[0, rs, :] = _layer_norm(alpha * x + (0.5 * gt_ref[0]) * z2) * lg_ref[...] + lb_ref[...]
        yield

    slabs = [slab(r0) for r0 in range(0, tm, rows)]
    for _ in range(5):
        for sb in slabs:
            next(sb)


def _pool_inv_counts(T):
    t = jnp.arange(T)
    rows = []
    for w in POOL_WINDOWS:
        cnt = jnp.clip(t - w // 2 + w, 0, T) - jnp.clip(t - w // 2, 0, T)
        rows.append(1.0 / cnt.astype(F32))
    return jnp.broadcast_to(jnp.stack(rows)[:, :, None], (len(POOL_WINDOWS), T, LANES))


def _merge(x, shift, scale, gate, a, hm, om, at, wg, pw, ps, nw, wb, wo, lg, lb, tm, alpha):
    B, T, D = x.shape
    per_b = shift.shape[0] > 1
    mod_map = (lambda b, i: (b, 0, 0)) if per_b else (lambda b, i: (0, 0, 0))
    tok = lambda n: pl.BlockSpec((1, tm, n), lambda b, i: (b, i, 0))
    full = lambda arr: pl.BlockSpec(arr.shape, lambda b, i: (0,) * arr.ndim)
    r = tm // HALO
    nh = T // HALO
    prev = pl.BlockSpec((1, HALO, BW), lambda b, i: (b, jnp.maximum(i * r - 1, 0), 0))
    nxt = pl.BlockSpec((1, HALO, BW), lambda b, i: (b, jnp.minimum((i + 1) * r, nh - 1), 0))
    mod = pl.BlockSpec((1, 1, D), mod_map)
    return pl.pallas_call(
        functools.partial(_merge_kernel, alpha=alpha),
        out_shape=jax.ShapeDtypeStruct((B, T, D), F32),
        grid=(B, T // tm),
        in_specs=[tok(D), mod, mod, mod, tok(BW), prev, nxt,
                  pl.BlockSpec((len(POOL_WINDOWS), tm, LANES), lambda b, i: (0, i, 0)),
                  tok(BW), tok(BW), tok(BW),
                  full(wg), full(pw), full(ps), full(nw), full(wb), full(wo), full(lg), full(lb)],
        out_specs=tok(D),
        scratch_shapes=[pltpu.VMEM((tm + 2 * HALO, BW), F32)],
        compiler_params=_cparams(2),
        name="merge",
    )(x, shift, scale, gate, a, a, a, _pool_inv_counts(T), hm, om, at, wg, pw, ps, nw, wb, wo, lg, lb)


def _ffn_kernel(x_ref, sh_ref, sc_ref, gt_ref, wg_ref, wu_ref, wd_ref, lg_ref, lb_ref, o_ref, *, alpha, fc):
    tm = x_ref.shape[1]
    rows = min(tm, FFN_SLAB)
    dff = wg_ref.shape[1]
    n = dff // fc

    def slab(r0):
        rs = slice(r0, r0 + rows)
        x = x_ref[0, rs, :]
        h = _modulated(x, sh_ref[0], sc_ref[0]).astype(BF16)
        yield

        def gate_up(j):
            cs = slice(j * fc, (j + 1) * fc)
            return _dot(h, wg_ref[:, cs]), _dot(h, wu_ref[:, cs])

        acc = None
        nxt = gate_up(0)
        for j in range(n):
            hg, up = nxt
            if j + 1 < n:
                nxt = gate_up(j + 1)
            act = (hg * (jnp.tanh(hg) + 1.0) * up).astype(BF16)
            part = _dot(act, wd_ref[j * fc:(j + 1) * fc, :])
            acc = part if acc is None else acc + part
            yield
        o_ref[0, rs, :] = _layer_norm(alpha * x + gt_ref[0] * acc) * lg_ref[...] + lb_ref[...]
        yield

    slabs = [slab(r0) for r0 in range(0, tm, rows)]
    for _ in range(n + 2):
        for sb in slabs:
            next(sb)


def _ffn(x, shift, scale, gate, wg, wu, wd, lg, lb, tm, alpha):
    B, T, D = x.shape
    per_b = shift.shape[0] > 1
    mod_map = (lambda b, i: (b, 0, 0)) if per_b else (lambda b, i: (0, 0, 0))
    tok = pl.BlockSpec((1, tm, D), lambda b, i: (b, i, 0))
    full = lambda arr: pl.BlockSpec(arr.shape, lambda b, i: (0,) * arr.ndim, pipeline_mode=pl.Buffered(1))
    mod = pl.BlockSpec((1, 1, D), mod_map)
    dff = wg.shape[1]
    fc = 2 * LANES if dff % (2 * LANES) == 0 else dff
    return pl.pallas_call(
        functools.partial(_ffn_kernel, alpha=alpha, fc=fc),
        out_shape=jax.ShapeDtypeStruct((B, T, D), F32),
        grid=(B, T // tm),
        in_specs=[tok, mod, mod, mod, full(wg), full(wu), full(wd), full(lg), full(lb)],
        out_specs=tok,
        compiler_params=_cparams(2),
        name="ffn",
    )(x, shift, scale, gate, wg, wu, wd, lg, lb)


def _rope_tables(T):
    t = jnp.arange(T)
    row = (t // GRID_W).astype(F32)
    col = (t % GRID_W).astype(F32)
    inv = ROPE_BASE ** (-jnp.arange(ROPE_PAIRS, dtype=F32) / ROPE_PAIRS)
    ang_r = row[:, None] * inv
    ang_c = col[:, None] * inv
    ang = jnp.concatenate([ang_r, ang_r, ang_c, ang_c], axis=-1)
    sign = jnp.concatenate([-jnp.ones(ROPE_PAIRS), jnp.ones(ROPE_PAIRS)] * 2).astype(F32)
    cos = jnp.tile(jnp.cos(ang), (1, 2))
    sin = jnp.tile(jnp.sin(ang) * sign, (1, 2))
    return cos, sin


def _pack_w_in(w_in_l):
    offs = np.cumsum([BW, BW, BW, BW, BW, 2 * M_HEADS, 2 * M_HEADS, BW, 2 * A_HDIM, 2 * A_HDIM])
    a, qm, km, vm, om, ig, fg, qa, ka, va, gates = jnp.split(w_in_l, offs.tolist(), axis=1)
    w_main = jnp.concatenate([a, qm, vm, om, qa, ka, va], axis=1).astype(BF16)
    h = M_HEADS
    w_gate_t = jnp.concatenate([ig[:, :h], fg[:, :h], ig[:, h:], fg[:, h:]], axis=1).T.astype(BF16)
    return w_main, km.T.astype(BF16), w_gate_t, (0.5 * gates).astype(BF16)


def kernel(x, c, ctx, c_ctx, w_mod, b_mod, w_in, pool_w, pool_scale, m_b_i, m_b_f, m_norm_w, attn_sink,
           w_branch, w_out, ln1_g, ln1_b, ln2_g, ln2_b, w_ffn_gate, w_ffn_up, w_ffn_down):
    B, T, D = x.shape
    C = ctx.shape[1]
    depth = w_mod.shape[0]
    alpha = (2 * depth) ** 0.25
    tm_x = 512 if T % 512 == 0 else T
    tm_f = 1024 if T % 1024 == 0 else tm_x
    tm_c = C
    h = M_HEADS

    rows = ((B + 1 + 7) // 8) * 8
    cc = jnp.zeros((rows, D), F32).at[:B].set(c).at[B].set(c_ctx)
    mods = _modulation(cc, w_mod, b_mod)
    cos_x, sin_x = _rope_tables(T)
    cos_c, sin_c = jnp.ones((C, LANES), F32), jnp.zeros((C, LANES), F32)
    zero_c = jnp.zeros((B, 2, h, M_HDIM, 2 * M_HDIM), F32)
    zero_m = jnp.zeros((B, 2, 8, LANES), F32)

    for l in range(depth):
        last = l == depth - 1
        mx = [mods[l, :B, j * D:(j + 1) * D].reshape(B, 1, D) for j in range(6)]
        mc = [mods[l, B:B + 1, j * D:(j + 1) * D].reshape(1, 1, D) for j in range(6)]
        w_main, w_kt, w_gate_t, w_bgate = _pack_w_in(w_in[l])
        bi, bf = m_b_i[l], m_b_f[l]
        bg = jnp.concatenate([bi[0], bf[0], bi[1], bf[1]]).reshape(16, 1)
        row = lambda v: v.reshape(1, -1)

        px = _project(x, mx[0], mx[1], cos_x, sin_x, w_main, w_kt, w_gate_t, bg, tm_f)
        pc = _project(ctx, mc[0], mc[1], cos_c, sin_c, w_main, w_kt, w_gate_t, bg, tm_c)
        a_x, qm_x, kt_x, vm_x, om_x, qa_x, ka_x, va_x, g_x = px
        a_c, qm_c, kt_c, vm_c, om_c, qa_c, ka_c, va_c, g_c = pc

        ctx_attn = None if last else (qa_c, ka_c, va_c, ka_c, va_c, attn_sink[l], False)
        mix_c = _mixers(qm_c, kt_c, vm_c, g_c, zero_c, zero_m, ctx_attn)
        h_c, st_c, st_m = mix_c[:3]
        h_x, _, _, at_x = _mixers(qm_x, kt_x, vm_x, g_x, st_c, st_m,
                                  (qa_x, ka_x, va_x, ka_c, va_c, attn_sink[l], True))

        wargs = (w_bgate, pool_w[l].astype(BF16), row(pool_scale[l]), row(m_norm_w[l]),
                 w_branch[l].astype(BF16), w_out[l].astype(BF16), row(ln1_g[l]), row(ln1_b[l]))
        fargs = ((0.5 * w_ffn_gate[l]).astype(BF16), w_ffn_up[l].astype(BF16), w_ffn_down[l].astype(BF16),
                 row(ln2_g[l]), row(ln2_b[l]))
        x1 = _merge(x, mx[0], mx[1], mx[2], a_x, h_x, om_x, at_x, *wargs, tm_x, alpha)
        x_new = _ffn(x1, mx[3], mx[4], mx[5], *fargs, tm_f, alpha)
        if not last:
            at_c = mix_c[3]
            c1 = _merge(ctx, mc[0], mc[1], mc[2], a_c, h_c, om_c, at_c, *wargs, tm_c, alpha)
            ctx = _ffn(c1, mc[3], mc[4], mc[5], *fargs, tm_c, alpha)
        x = x_new
    return x
```
